```python
import jax, jax.numpy as jnp
from jax import lax
import numpy as np

D_MODEL = 1024
BATCH = 8
SEQ = 8192
DEPTH = 2

GRID_W = 64
EPS = 1e-6

MLA_HEADS = 8
MLA_Q_LORA = 256
MLA_KV_LORA = 128
MLA_NOPE = 64
MLA_ROPE = 32
MLA_V = 64
MLA_QK = MLA_NOPE + MLA_ROPE
ROPE_THETA = 10000.0
Q_BLOCK = 128

NA_HEADS = 8
NA_HD = 64
NA_KH = 8
NA_KW = 16

IN_COLS = MLA_Q_LORA + MLA_KV_LORA + MLA_ROPE + 3 * NA_HEADS * NA_HD
MIX_WIDTH = MLA_HEADS * MLA_V + NA_HEADS * NA_HD

CONV_W = 31

N_EXPERTS = 64
TOP_K = 8
N_GROUPS = 8
TOPK_GROUPS = 4
D_EXPERT = 256
D_SHARED = 256
ROUTED_SCALE = 2.5
MOE_BLOCK = 256

kernel_name = "hybrid_mla_natten_conformer_moe_encoder"


def rms_norm(x, w):
    xf = x.astype(jnp.float32)
    y = xf * lax.rsqrt(jnp.mean(xf * xf, axis=-1, keepdims=True) + EPS)
    return (y * w.astype(jnp.float32)).astype(x.dtype)


def layer_norm(x, w, b):
    xf = x.astype(jnp.float32)
    mu = jnp.mean(xf, axis=-1, keepdims=True)
    var = jnp.mean(jnp.square(xf - mu), axis=-1, keepdims=True)
    y = (xf - mu) * lax.rsqrt(var + EPS)
    return (y * w.astype(jnp.float32) + b.astype(jnp.float32)).astype(x.dtype)


def axial_rope(seq_len, dtype):
    t = jnp.arange(seq_len)
    row = (t // GRID_W).astype(jnp.float32)
    col = (t % GRID_W).astype(jnp.float32)
    n_freq = MLA_ROPE // 4
    inv = ROPE_THETA ** (-jnp.arange(n_freq, dtype=jnp.float32) / n_freq)
    ang = jnp.concatenate([row[:, None] * inv, col[:, None] * inv], axis=-1)
    return jnp.cos(ang).astype(dtype), jnp.sin(ang).astype(dtype)


def rope_tail(x, cos, sin):
    xn, xr = x[..., :-MLA_ROPE], x[..., -MLA_ROPE:]
    half = MLA_ROPE // 2
    x1, x2 = xr[..., :half], xr[..., half:]
    c, s = cos[None, :, None, :], sin[None, :, None, :]
    rot = jnp.concatenate([x1 * c - x2 * s, x1 * s + x2 * c], axis=-1)
    return jnp.concatenate([xn, rot], axis=-1)


def dense_bidirectional_attention(q, k, v, scale):
    B, S, H, Dk = q.shape
    Dv = v.shape[-1]
    nb = S // Q_BLOCK
    q_blocks = q.reshape(B, nb, Q_BLOCK, H, Dk).transpose(1, 0, 2, 3, 4)

    def one_block(qb):
        s = jnp.einsum('bqhd,bkhd->bhqk', qb, k, preferred_element_type=jnp.float32) * scale
        p = jax.nn.softmax(s, axis=-1).astype(v.dtype)
        return jnp.einsum('bhqk,bkhd->bqhd', p, v)

    o = lax.map(one_block, q_blocks)
    return o.transpose(1, 0, 2, 3, 4).reshape(B, S, H * Dv)


def mla_heads(cq, ckv, k_pe, q_a_norm_w, kv_a_norm_w, w_q_b, w_kv_b, q_norm_w, k_norm_w):
    B, S, _ = cq.shape
    cos, sin = axial_rope(S, cq.dtype)
    q = (rms_norm(cq, q_a_norm_w) @ w_q_b).reshape(B, S, MLA_HEADS, MLA_QK)
    kv = (rms_norm(ckv, kv_a_norm_w) @ w_kv_b).reshape(B, S, MLA_HEADS, MLA_NOPE + MLA_V)
    k_nope, v = kv[..., :MLA_NOPE], kv[..., MLA_NOPE:]
    k_pe_h = jnp.broadcast_to(k_pe[:, :, None, :], (B, S, MLA_HEADS, MLA_ROPE))
    k = jnp.concatenate([k_nope, k_pe_h], axis=-1)
    q = rope_tail(rms_norm(q, q_norm_w), cos, sin)
    k = rope_tail(rms_norm(k, k_norm_w), cos, sin)
    return dense_bidirectional_attention(q, k, v, MLA_QK ** -0.5)


def neighbourhood_attention(q, k, v, rpb):
    B, S, H, Dh = q.shape
    rows = S // GRID_W
    kh = min(NA_KH, rows)
    qg = q.reshape(B, rows, GRID_W, H, Dh)
    kg = k.reshape(B, rows, GRID_W, H, Dh)
    vg = v.reshape(B, rows, GRID_W, H, Dh)
    cols = jnp.arange(GRID_W)
    col_start = jnp.clip(cols - NA_KW // 2, 0, GRID_W - NA_KW)
    col_idx = col_start[:, None] + jnp.arange(NA_KW)[None, :]
    col_off = col_idx - cols[:, None] + (NA_KW - 1)
    scale = Dh ** -0.5

    def one_row(r):
        rs = jnp.clip(r - kh // 2, 0, rows - kh)
        q_r = lax.dynamic_index_in_dim(qg, r, axis=1, keepdims=False)
        k_rows = lax.dynamic_slice_in_dim(kg, rs, kh, axis=1)
        v_rows = lax.dynamic_slice_in_dim(vg, rs, kh, axis=1)
        k_win = k_rows[:, :, col_idx]
        v_win = v_rows[:, :, col_idx]
        row_off = rs + jnp.arange(kh) - r + (NA_KH - 1)
        bias = rpb[:, row_off][:, :, col_off].transpose(0, 2, 1, 3)
        s = jnp.einsum('bchd,bicjhd->bhcij', q_r, k_win,
                       preferred_element_type=jnp.float32) * scale
        s = s + bias[None].astype(jnp.float32)
        p = jax.nn.softmax(s.reshape(B, H, GRID_W, kh * NA_KW), axis=-1)
        p = p.reshape(B, H, GRID_W, kh, NA_KW).astype(v.dtype)
        return jnp.einsum('bhcij,bicjhd->bchd', p, v_win)

    o = lax.map(one_row, jnp.arange(rows))
    return o.transpose(1, 0, 2, 3, 4).reshape(B, S, H * Dh)


def attention_mixer(h, w_in, q_a_norm_w, kv_a_norm_w, w_q_b, w_kv_b, mla_q_norm_w,
                    mla_k_norm_w, na_q_norm_w, na_k_norm_w, na_rpb, w_out):
    B, S, _ = h.shape
    p = h @ w_in
    o1 = MLA_Q_LORA
    o2 = o1 + MLA_KV_LORA
    o3 = o2 + MLA_ROPE
    cq, ckv, k_pe, qkv_na = p[..., :o1], p[..., o1:o2], p[..., o2:o3], p[..., o3:]
    mla_out = mla_heads(cq, ckv, k_pe, q_a_norm_w, kv_a_norm_w, w_q_b, w_kv_b,
                        mla_q_norm_w, mla_k_norm_w)
    qkv = qkv_na.reshape(B, S, 3, NA_HEADS, NA_HD)
    q_na = rms_norm(qkv[:, :, 0], na_q_norm_w)
    k_na = rms_norm(qkv[:, :, 1], na_k_norm_w)
    na_out = neighbourhood_attention(q_na, k_na, qkv[:, :, 2], na_rpb)
    return jnp.concatenate([mla_out, na_out], axis=-1) @ w_out


def conformer_conv(h, pw1_w, pw1_b, dw_w, dw_b, ln_w, ln_b, pw2_w, pw2_b):
    a = h @ pw1_w + pw1_b
    g = a[..., :D_MODEL] * jax.nn.sigmoid(a[..., D_MODEL:])
    y = lax.conv_general_dilated(g, dw_w[:, None, :], window_strides=(1,),
                                 padding=[(CONV_W // 2, CONV_W // 2)],
                                 dimension_numbers=('NWC', 'WIO', 'NWC'),
                                 feature_group_count=D_MODEL) + dw_b
    y = jax.nn.silu(layer_norm(y, ln_w, ln_b))
    return y @ pw2_w + pw2_b


def moe_ffn(h, router_w, router_b, w_gate, w_up, w_down, ws_gate, ws_up, ws_down):
    B, S, D = h.shape
    T = B * S
    hf = h.reshape(T, D)
    scores = jax.nn.sigmoid(jnp.dot(hf.astype(jnp.float32), router_w.astype(jnp.float32)))
    sel = scores + router_b.astype(jnp.float32)
    grp = sel.reshape(T, N_GROUPS, N_EXPERTS // N_GROUPS)
    grp_score = lax.top_k(grp, 2)[0].sum(-1)
    _, gidx = lax.top_k(grp_score, TOPK_GROUPS)
    gmask = (gidx[..., None] == jnp.arange(N_GROUPS)).any(axis=-2)
    emask = jnp.repeat(gmask, N_EXPERTS // N_GROUPS, axis=-1)
    _, idx = lax.top_k(jnp.where(emask, sel, -jnp.inf), TOP_K)
    gates = jnp.take_along_axis(scores, idx, axis=-1)
    gates = gates / gates.sum(-1, keepdims=True) * ROUTED_SCALE

    A = T * TOP_K
    flat_e = idx.reshape(A)
    flat_tok = jnp.arange(A, dtype=jnp.int32) // TOP_K
    flat_g = gates.reshape(A)
    order = jnp.argsort(flat_e)
    se, stok, sg = flat_e[order], flat_tok[order], flat_g[order]
    counts = jnp.bincount(flat_e, length=N_EXPERTS)
    padded = (counts + MOE_BLOCK - 1) // MOE_BLOCK * MOE_BLOCK
    pend = jnp.cumsum(padded)
    pstart = pend - padded
    start = jnp.cumsum(counts) - counts
    dest = pstart[se] + jnp.arange(A, dtype=jnp.int32) - start[se]
    n_blocks = A // MOE_BLOCK + N_EXPERTS
    buf_tok = jnp.full((n_blocks * MOE_BLOCK,), T, jnp.int32).at[dest].set(stok)
    buf_g = jnp.zeros((n_blocks * MOE_BLOCK,), jnp.float32).at[dest].set(sg)
    blk_e = jnp.minimum(jnp.searchsorted(pend, jnp.arange(n_blocks) * MOE_BLOCK, side='right'),
                        N_EXPERTS - 1)
    h_pad = jnp.concatenate([hf, jnp.zeros((1, D), hf.dtype)], axis=0)

    def expert_block(y, blk):
        tok, g, e = blk
        xb = h_pad[tok]
        u = jax.nn.silu(xb @ w_gate[e]) * (xb @ w_up[e])
        out = u @ w_down[e]
        return y.at[tok].add(out * g[:, None].astype(out.dtype)), None

    y, _ = lax.scan(expert_block, jnp.zeros((T + 1, D), hf.dtype),
                    (buf_tok.reshape(n_blocks, MOE_BLOCK),
                     buf_g.reshape(n_blocks, MOE_BLOCK), blk_e))
    shared = (jax.nn.silu(hf @ ws_gate) * (hf @ ws_up)) @ ws_down
    return (y[:T] + shared).reshape(B, S, D)


def setup_inputs(seed: int = 0) -> dict:
    key = jax.random.key(seed)
    ks = iter(jax.random.split(key, 48))
    D = D_MODEL
    n_att = (DEPTH + 1) // 2
    n_conv = DEPTH // 2

    def nrm(shape, scale):
        return jax.random.normal(next(ks), shape, jnp.float32) * scale

    def gain(shape):
        return 1.0 + nrm(shape, 0.1)

    return {
        "x": nrm((BATCH, SEQ, D), 1.0),
        "c": nrm((BATCH, D), 1.0),
        "norm1_w": gain((DEPTH, D)),
        "norm2_w": gain((DEPTH, D)),
        "ada_w": nrm((DEPTH, D, 6 * D), 0.5 * D ** -0.5),
        "ada_b": nrm((DEPTH, 6 * D), 0.02),
        "w_in": nrm((n_att, D, IN_COLS), D ** -0.5),
        "q_a_norm_w": gain((n_att, MLA_Q_LORA)),
        "kv_a_norm_w": gain((n_att, MLA_KV_LORA)),
        "w_q_b": nrm((n_att, MLA_Q_LORA, MLA_HEADS * MLA_QK), MLA_Q_LORA ** -0.5),
        "w_kv_b": nrm((n_att, MLA_KV_LORA, MLA_HEADS * (MLA_NOPE + MLA_V)), MLA_KV_LORA ** -0.5),
        "mla_q_norm_w": gain((n_att, MLA_QK)),
        "mla_k_norm_w": gain((n_att, MLA_QK)),
        "na_q_norm_w": gain((n_att, NA_HD)),
        "na_k_norm_w": gain((n_att, NA_HD)),
        "na_rpb": nrm((n_att, NA_HEADS, 2 * NA_KH - 1, 2 * NA_KW - 1), 0.1),
        "w_out": nrm((n_att, MIX_WIDTH, D), MIX_WIDTH ** -0.5),
        "pw1_w": nrm((n_conv, D, 2 * D), D ** -0.5),
        "pw1_b": nrm((n_conv, 2 * D), 0.02),
        "dw_w": nrm((n_conv, CONV_W, D), CONV_W ** -0.5),
        "dw_b": nrm((n_conv, D), 0.02),
        "conv_ln_w": gain((n_conv, D)),
        "conv_ln_b": nrm((n_conv, D), 0.02),
        "pw2_w": nrm((n_conv, D, D), D ** -0.5),
        "pw2_b": nrm((n_conv, D), 0.02),
        "router_w": nrm((DEPTH, D, N_EXPERTS), D ** -0.5),
        "router_b": nrm((DEPTH, N_EXPERTS), 0.01),
        "exp_w_gate": nrm((DEPTH, N_EXPERTS, D, D_EXPERT), D ** -0.5),
        "exp_w_up": nrm((DEPTH, N_EXPERTS, D, D_EXPERT), D ** -0.5),
        "exp_w_down": nrm((DEPTH, N_EXPERTS, D_EXPERT, D), D_EXPERT ** -0.5),
        "sh_w_gate": nrm((DEPTH, D, D_SHARED), D ** -0.5),
        "sh_w_up": nrm((DEPTH, D, D_SHARED), D ** -0.5),
        "sh_w_down": nrm((DEPTH, D_SHARED, D), D_SHARED ** -0.5),
    }


def reference(x, c, norm1_w, norm2_w, ada_w, ada_b, w_in, q_a_norm_w, kv_a_norm_w,
              w_q_b, w_kv_b, mla_q_norm_w, mla_k_norm_w, na_q_norm_w, na_k_norm_w,
              na_rpb, w_out, pw1_w, pw1_b, dw_w, dw_b, conv_ln_w, conv_ln_b, pw2_w,
              pw2_b, router_w, router_b, exp_w_gate, exp_w_up, exp_w_down,
              sh_w_gate, sh_w_up, sh_w_down):
    c_act = jax.nn.silu(c)
    for l in range(DEPTH):
        mod = (c_act @ ada_w[l] + ada_b[l])[:, None, :]
        sh1, sc1, g1, sh2, sc2, g2 = jnp.split(mod, 6, axis=-1)
        h = rms_norm(x, norm1_w[l]) * (1 + sc1) + sh1
        i = l // 2
        if l % 2 == 0:
            mix = attention_mixer(h, w_in[i], q_a_norm_w[i], kv_a_norm_w[i], w_q_b[i],
                                  w_kv_b[i], mla_q_norm_w[i], mla_k_norm_w[i],
                                  na_q_norm_w[i], na_k_norm_w[i], na_rpb[i], w_out[i])
        else:
            mix = conformer_conv(h, pw1_w[i], pw1_b[i], dw_w[i], dw_b[i], conv_ln_w[i],
                                 conv_ln_b[i], pw2_w[i], pw2_b[i])
        x = x + g1 * mix
        h = rms_norm(x, norm2_w[l]) * (1 + sc2) + sh2
        x = x + g2 * moe_ffn(h, router_w[l], router_b[l], exp_w_gate[l], exp_w_up[l],
                             exp_w_down[l], sh_w_gate[l], sh_w_up[l], sh_w_down[l])
    return x
```

```python
import functools
import math

import jax
import jax.numpy as jnp
from jax import lax
from jax.experimental import pallas as pl
from jax.experimental.pallas import tpu as pltpu

F32 = jnp.float32
BF16 = jnp.bfloat16

EPS = 1e-6
GRID_W = 64
LANES = 128
HEAD_PAD = 128

MLA_HEADS = 8
MLA_Q_LORA = 256
MLA_KV_LORA = 128
MLA_NOPE = 64
MLA_ROPE = 32
MLA_V = 64
MLA_QK = MLA_NOPE + MLA_ROPE
ROPE_THETA = 10000.0

NA_HEADS = 8
NA_HD = 64
NA_KH = 8
NA_KW = 16
NA_QROWS = 8
NA_KROWS = 16

CONV_W = 31
CONV_HALO = 16

N_EXPERTS = 64
TOP_K = 8
N_GROUPS = 8
TOPK_GROUPS = 4
GROUP_SIZE = N_EXPERTS // N_GROUPS
D_EXPERT = 256
ROUTED_SCALE = 2.5
MOE_BLOCK = 512

NEG_BIG = -1e30
VMEM_LIMIT = 56 * 1024 * 1024


def _cparams(n_axes):
    return pltpu.CompilerParams(dimension_semantics=("arbitrary",) * n_axes,
                                vmem_limit_bytes=VMEM_LIMIT)


def _dot(a, b):
    return jnp.dot(a, b, preferred_element_type=F32)


def _dot_nt(a, b):
    return lax.dot_general(a, b, (((1,), (1,)), ((), ())), preferred_element_type=F32)


def _rms(x):
    return x * lax.rsqrt(jnp.mean(x * x, axis=-1, keepdims=True) + EPS)


def _modnorm(x, w, sc, sh):
    return _rms(x) * w * (1.0 + sc) + sh


def _sigmoid(x):
    return 1.0 / (1.0 + jnp.exp(-x))


def _silu(x):
    return x * _sigmoid(x)


def _ada_kernel(c_ref, w_ref, b_ref, o_ref):
    c = c_ref[...]
    ca = _silu(c)
    o_ref[0] = jnp.dot(ca, w_ref[0], preferred_element_type=F32,
                       precision=lax.Precision.HIGHEST) + b_ref[0]


def _ada_mod(c, ada_w, ada_b):
    depth, d, d6 = ada_w.shape
    b = c.shape[0]
    cn = d
    return pl.pallas_call(
        _ada_kernel,
        grid=(depth, d6 // cn),
        in_specs=[pl.BlockSpec((b, d), lambda l, j: (0, 0)),
                  pl.BlockSpec((1, d, cn), lambda l, j: (l, 0, j)),
                  pl.BlockSpec((1, 1, cn), lambda l, j: (l, 0, j))],
        out_specs=pl.BlockSpec((1, b, cn), lambda l, j: (l, 0, j)),
        out_shape=jax.ShapeDtypeStruct((depth, b, d6), F32),
        compiler_params=_cparams(2),
        name="ada_mod",
    )(c, ada_w, ada_b.reshape(depth, 1, d6))


def _attn_pre_kernel(x_ref, sh_ref, sc_ref, n1_ref, win_ref, qaw_ref, kvaw_ref, wqa_ref, wqb_ref,
                     wk_ref, wv_ref, cq_ref, sq_ref, ck_ref, sk_ref, naq_ref, nak_ref,
                     qm_ref, km_ref, vm_ref, qn_ref, kn_ref, vn_ref):
    x = x_ref[0]
    h = _modnorm(x, n1_ref[...], sc_ref[0], sh_ref[0])
    p = _dot(h.astype(BF16), win_ref[...])
    o = 0
    cq = p[:, o:o + MLA_Q_LORA]; o += MLA_Q_LORA
    ckv = p[:, o:o + MLA_KV_LORA]; o += MLA_KV_LORA
    pe = p[:, o:o + HEAD_PAD]; o += HEAD_PAD
    pes = p[:, o:o + HEAD_PAD]; o += HEAD_PAD
    nw = NA_HEADS * NA_HD
    qn = p[:, o:o + nw]; o += nw
    kn = p[:, o:o + nw]; o += nw
    vn = p[:, o:o + nw]

    lane = lax.broadcasted_iota(jnp.int32, (1, HEAD_PAD), 1)

    cqn = (_rms(cq) * qaw_ref[...]).astype(BF16)
    qa = _dot(cqn, wqa_ref[...])
    qb = _dot(cqn, wqb_ref[...])
    cq_t, sq_t = cq_ref[...], sq_ref[...]
    for hh in range(MLA_HEADS):
        sl = slice(hh * HEAD_PAD, (hh + 1) * HEAD_PAD)
        a = qa[:, sl]
        r = lax.rsqrt(jnp.sum(a * a, axis=-1, keepdims=True) * (1.0 / MLA_QK) + EPS)
        qm_ref[0, :, sl] = (r * (a * cq_t + qb[:, sl] * sq_t)).astype(BF16)

    ckvn = (_rms(ckv) * kvaw_ref[...]).astype(BF16)
    kk = _dot(ckvn, wk_ref[...])
    vv = _dot(ckvn, wv_ref[...])
    ck_t, sk_t = ck_ref[...], sk_ref[...]
    pe_b = pes * sk_t
    one_col = jnp.where(lane == MLA_V, 1.0, 0.0)
    for hh in range(MLA_HEADS):
        sl = slice(hh * HEAD_PAD, (hh + 1) * HEAD_PAD)
        a = kk[:, sl] + pe
        r = lax.rsqrt(jnp.sum(a * a, axis=-1, keepdims=True) * (1.0 / MLA_QK) + EPS)
        km_ref[0, :, sl] = (r * (a * ck_t + pe_b)).astype(BF16)
        vm_ref[0, :, sl] = (vv[:, sl] + one_col).astype(BF16)

    lo = lane < NA_HD
    for src, w_ref, dst in ((qn, naq_ref, qn_ref), (kn, nak_ref, kn_ref)):
        for g in range(nw // LANES):
            sl = slice(g * LANES, (g + 1) * LANES)
            a = src[:, sl]
            sq = a * a
            tot = jnp.sum(sq, axis=-1, keepdims=True)
            s_lo = jnp.sum(jnp.where(lo, sq, 0.0), axis=-1, keepdims=True)
            r_lo = lax.rsqrt(s_lo * (1.0 / NA_HD) + EPS)
            r_hi = lax.rsqrt((tot - s_lo) * (1.0 / NA_HD) + EPS)
            dst[0, :, sl] = (a * jnp.where(lo, r_lo, r_hi) * w_ref[:, sl]).astype(BF16)
    vn_ref[0] = vn.astype(BF16)


def _attn_pre(x, sh, sc, n1w, prm, tm):
    b, s, d = x.shape
    hw = MLA_HEADS * HEAD_PAD
    nw = NA_HEADS * NA_HD
    full = lambda a: pl.BlockSpec(a.shape, lambda bi, si: (0,) * a.ndim)
    mod = pl.BlockSpec((1, 1, d), lambda bi, si: (bi, 0, 0))
    tab = pl.BlockSpec((tm, HEAD_PAD), lambda bi, si: (si, 0))
    tok = lambda w: pl.BlockSpec((1, tm, w), lambda bi, si: (bi, si, 0))
    ws = [prm[k] for k in ("w_in", "qa_w", "kva_w", "wq_a", "wq_b", "wk", "wv")]
    ins = [x, sh, sc, n1w] + ws + [prm["cq"], prm["sq"], prm["ck"], prm["sk"], prm["na_qw"], prm["na_kw"]]
    in_specs = ([tok(d), mod, mod, full(n1w)] + [full(a) for a in ws] + [tab] * 4
                + [full(prm["na_qw"]), full(prm["na_kw"])])
    out_shape = ([jax.ShapeDtypeStruct((b, s, hw), BF16)] * 3 + [jax.ShapeDtypeStruct((b, s, nw), BF16)] * 3)
    out_specs = [tok(hw)] * 3 + [tok(nw)] * 3
    return pl.pallas_call(
        _attn_pre_kernel, grid=(b, s // tm), in_specs=in_specs, out_specs=out_specs,
        out_shape=out_shape, compiler_params=_cparams(2), name="attn_pre",
    )(*ins)


def _mla_flash_kernel(q_ref, k_ref, v_ref, o_ref, *, tk):
    s_len = k_ref.shape[1]
    tq = q_ref.shape[1]
    outs = []
    for hh in range(2):
        sl = slice(hh * HEAD_PAD, (hh + 1) * HEAD_PAD)
        q = q_ref[0, :, sl]

        def body(j, carry):
            m, acc = carry
            off = pl.multiple_of(j * tk, tk)
            k = k_ref[0, pl.ds(off, tk), sl]
            v = v_ref[0, pl.ds(off, tk), sl]
            s = _dot_nt(q, k)
            m_new = jnp.maximum(m, jnp.max(s, axis=-1, keepdims=True))
            p = jnp.exp2(s - m_new)
            alpha = jnp.exp2(m - m_new)
            acc = alpha * acc + _dot(p.astype(BF16), v)
            return m_new, acc

        m0 = jnp.full((tq, 1), -jnp.inf, F32)
        a0 = jnp.zeros((tq, HEAD_PAD), F32)
        _, acc = lax.fori_loop(0, s_len // tk, body, (m0, a0))
        outs.append(acc / acc[:, MLA_V:MLA_V + 1])
    lane = lax.broadcasted_iota(jnp.int32, (1, HEAD_PAD), 1)
    shifted = pltpu.roll(outs[1], MLA_V, axis=1)
    o_ref[0] = jnp.where(lane < MLA_V, outs[0], shifted).astype(BF16)


def _mla_flash(q, k, v, tq, tk):
    b, s, hw = q.shape
    hp = MLA_HEADS // 2
    return pl.pallas_call(
        functools.partial(_mla_flash_kernel, tk=tk),
        grid=(b, hp, s // tq),
        in_specs=[pl.BlockSpec((1, tq, 2 * HEAD_PAD), lambda bi, h, qi: (bi, qi, h)),
                  pl.BlockSpec((1, s, 2 * HEAD_PAD), lambda bi, h, qi: (bi, 0, h)),
                  pl.BlockSpec((1, s, 2 * HEAD_PAD), lambda bi, h, qi: (bi, 0, h))],
        out_specs=pl.BlockSpec((1, tq, 2 * MLA_V), lambda bi, h, qi: (bi, qi, h)),
        out_shape=jax.ShapeDtypeStruct((b, s, MLA_HEADS * MLA_V), BF16),
        compiler_params=_cparams(3), name="mla_flash",
    )(q, k, v)


def _na_kernel(q_ref, k0, k1, k2, k3, v0, v1, v2, v3, t_ref, o_ref):
    q2 = q_ref[0]
    kc = jnp.concatenate([k0[0], k1[0], k2[0], k3[0]], axis=0)
    vc = jnp.concatenate([v0[0], v1[0], v2[0], v3[0]], axis=0)
    lane = lax.broadcasted_iota(jnp.int32, (1, LANES), 1)
    lo = lane < NA_HD
    outs = []
    for hh in range(2):
        qm = jnp.where(lo if hh == 0 else jnp.logical_not(lo), q2, jnp.zeros_like(q2))
        s = _dot_nt(qm, kc) + t_ref[0, hh]
        m = jnp.max(s, axis=-1, keepdims=True)
        p = jnp.exp(s - m)
        l = jnp.sum(p, axis=-1, keepdims=True)
        outs.append(_dot(p.astype(BF16), vc) / l)
    o_ref[0] = jnp.where(lo, outs[0], outs[1]).astype(BF16)


def _na_attention(q, k, v, table):
    b, s, nw = q.shape
    rows = s // GRID_W
    nblk = rows // NA_QROWS
    tq = NA_QROWS * GRID_W
    kp = 4
    tkp = NA_KROWS * GRID_W // kp
    kmax = s // tkp - kp
    hp = NA_HEADS // 2

    def kspec(j):
        return pl.BlockSpec((1, tkp, LANES),
                            lambda h, blk, bi: (bi, jnp.clip(2 * blk - 1, 0, kmax) + j, h))

    def tmap(h, blk, bi):
        cls = jnp.where(blk == 0, 0, jnp.where(blk == nblk - 1, 2, 1))
        return (cls, h, 0, 0)

    return pl.pallas_call(
        _na_kernel, grid=(hp, nblk, b),
        in_specs=[pl.BlockSpec((1, tq, LANES), lambda h, blk, bi: (bi, blk, h))]
        + [kspec(j) for j in range(kp)] * 2
        + [pl.BlockSpec((1, 2, tq, NA_KROWS * GRID_W), tmap)],
        out_specs=pl.BlockSpec((1, tq, LANES), lambda h, blk, bi: (bi, blk, h)),
        out_shape=jax.ShapeDtypeStruct((b, s, nw), BF16),
        compiler_params=_cparams(3), name="na_attn",
    )(q, k, k, k, k, v, v, v, v, table)


def _na_table(rpb, rows):
    nblk = rows // NA_QROWS
    tabs = []
    c = jnp.arange(GRID_W)
    cs = jnp.clip(c - NA_KW // 2, 0, GRID_W - NA_KW)
    vcol = (c[None, :] >= cs[:, None]) & (c[None, :] < cs[:, None] + NA_KW)
    co = jnp.clip(c[None, :] - c[:, None] + NA_KW - 1, 0, 2 * NA_KW - 2)
    for blk in (0, min(1, nblk - 1), nblk - 1):
        r = NA_QROWS * blk + jnp.arange(NA_QROWS)
        kb = min(max(NA_QROWS * blk - NA_KH // 2, 0), rows - NA_KROWS)
        krow = kb + jnp.arange(NA_KROWS)
        rs = jnp.clip(r - NA_KH // 2, 0, rows - NA_KH)
        vrow = (krow[None, :] >= rs[:, None]) & (krow[None, :] < rs[:, None] + NA_KH)
        ro = jnp.clip(krow[None, :] - r[:, None] + NA_KH - 1, 0, 2 * NA_KH - 2)
        bias = rpb[:, ro][:, :, :, co]
        bias = bias.transpose(0, 1, 3, 2, 4)
        valid = vrow[:, None, :, None] & vcol[None, :, None, :]
        tabs.append(jnp.where(valid[None], bias, NEG_BIG).reshape(
            rpb.shape[0], NA_QROWS * GRID_W, NA_KROWS * GRID_W))
    return jnp.stack(tabs).astype(F32)


def _bfly(x, op):
    for sh in (1, 2, 4):
        x = op(x, pltpu.roll(x, sh, axis=x.ndim - 2))
    return x


def _route(h2, rwh_ref, rwl_ref, rb_ref, idx_ref, gate_ref):
    tm = h2.shape[0]
    h_hi = h2.astype(BF16)
    h_lo = (h2 - h_hi.astype(F32)).astype(BF16)
    rwh = rwh_ref[...]
    logits = _dot_nt(rwh, h_hi) + _dot_nt(rwh, h_lo) + _dot_nt(rwl_ref[...], h_hi)
    scores = _sigmoid(logits)
    sel = scores + rb_ref[...]
    sub = lax.broadcasted_iota(jnp.int32, (GROUP_SIZE, tm), 0)
    sc_g = [scores[g * GROUP_SIZE:(g + 1) * GROUP_SIZE] for g in range(N_GROUPS)]
    sel_g = [sel[g * GROUP_SIZE:(g + 1) * GROUP_SIZE] for g in range(N_GROUPS)]

    gscore = []
    for x in sel_g:
        m1 = _bfly(x, jnp.maximum)
        first = _bfly(jnp.where(x == m1, sub, GROUP_SIZE), jnp.minimum)
        m2 = _bfly(jnp.where(sub == first, -jnp.inf, x), jnp.maximum)
        gscore.append(m1 + m2)

    gmask = [jnp.zeros((GROUP_SIZE, tm), jnp.bool_) for _ in range(N_GROUPS)]
    for _ in range(TOPK_GROUPS):
        best = functools.reduce(jnp.maximum, gscore)
        gidx = functools.reduce(jnp.minimum,
                                [jnp.where(gs == best, g, N_GROUPS) for g, gs in enumerate(gscore)])
        for g in range(N_GROUPS):
            hit = gidx == g
            gmask[g] = jnp.logical_or(gmask[g], hit)
            gscore[g] = jnp.where(hit, -jnp.inf, gscore[g])

    cand = [jnp.where(gmask[g], sel_g[g], -jnp.inf) for g in range(N_GROUPS)]
    eid = [sub + g * GROUP_SIZE for g in range(N_GROUPS)]
    idx_out = jnp.zeros((TOP_K, tm), jnp.int32)
    gate_out = jnp.zeros((TOP_K, tm), F32)
    gsum = jnp.zeros((GROUP_SIZE, tm), F32)
    for k in range(TOP_K):
        best = _bfly(functools.reduce(jnp.maximum, cand), jnp.maximum)
        pick = _bfly(functools.reduce(
            jnp.minimum, [jnp.where(cand[g] == best, eid[g], N_EXPERTS) for g in range(N_GROUPS)]),
            jnp.minimum)
        gv = _bfly(functools.reduce(
            jnp.add, [jnp.where(eid[g] == pick, sc_g[g], 0.0) for g in range(N_GROUPS)]), jnp.add)
        cand = [jnp.where(eid[g] == pick, -jnp.inf, cand[g]) for g in range(N_GROUPS)]
        idx_out = jnp.where(sub == k, pick, idx_out)
        gate_out = jnp.where(sub == k, gv, gate_out)
        gsum = gsum + gv
    idx_ref[...] = idx_out
    gate_ref[...] = gate_out / gsum * ROUTED_SCALE


def _tail(x, mix, g1, n2w, sc2, sh2, rwh_ref, rwl_ref, rb_ref, x1_ref, h2_ref, idx_ref, gate_ref):
    x1 = x + g1 * mix
    h2 = _modnorm(x1, n2w, sc2, sh2)
    x1_ref[...] = x1
    h2_ref[...] = h2
    _route(h2, rwh_ref, rwl_ref, rb_ref, idx_ref, gate_ref)


def _post_mix_kernel(x_ref, mla_ref, na_ref, wo_a_ref, wo_b_ref, g1_ref, n2_ref, sc2_ref, sh2_ref,
                     rwh_ref, rwl_ref, rb_ref, x1_ref, h2_ref, idx_ref, gate_ref):
    mix = _dot(mla_ref[...], wo_a_ref[...]) + _dot(na_ref[...], wo_b_ref[...])
    _tail(x_ref[...], mix, g1_ref[0], n2_ref[...], sc2_ref[0], sh2_ref[0],
          rwh_ref, rwl_ref, rb_ref, x1_ref, h2_ref, idx_ref, gate_ref)


def _tail_specs(t, d, tm, tiles_per_batch):
    tok = lambda w: pl.BlockSpec((tm, w), lambda i: (i, 0))
    mod = pl.BlockSpec((1, 1, d), lambda i: (i // tiles_per_batch, 0, 0))
    full2 = lambda shp: pl.BlockSpec(shp, lambda i: (0, 0))
    in_tail = [mod, full2((1, d)), mod, mod, full2((N_EXPERTS, d)), full2((N_EXPERTS, d)),
               full2((N_EXPERTS, 1))]
    out_specs = [tok(d), tok(d), pl.BlockSpec((TOP_K, tm), lambda i: (0, i)),
                 pl.BlockSpec((TOP_K, tm), lambda i: (0, i))]
    out_shape = [jax.ShapeDtypeStruct((t, d), F32), jax.ShapeDtypeStruct((t, d), F32),
                 jax.ShapeDtypeStruct((TOP_K, t), jnp.int32), jax.ShapeDtypeStruct((TOP_K, t), F32)]
    return tok, mod, full2, in_tail, out_specs, out_shape


def _post_mix(x2d, mla_o, na_o, wo_a, wo_b, g1, n2w, sc2, sh2, rwh, rwl, rb, seq, tm):
    t, d = x2d.shape
    tok, mod, full2, in_tail, out_specs, out_shape = _tail_specs(t, d, tm, seq // tm)
    return pl.pallas_call(
        _post_mix_kernel, grid=(t // tm,),
        in_specs=[tok(d), tok(mla_o.shape[1]), tok(na_o.shape[1]), full2(wo_a.shape), full2(wo_b.shape)]
        + in_tail,
        out_specs=out_specs, out_shape=out_shape, compiler_params=_cparams(1), name="post_mix",
    )(x2d, mla_o, na_o, wo_a, wo_b, g1, n2w, sc2, sh2, rwh, rwl, rb)


def _conv_pre_kernel(x_ref, sh_ref, sc_ref, n1_ref, w_ref, b_ref, g_ref):
    d = x_ref.shape[-1]
    h = _modnorm(x_ref[0], n1_ref[...], sc_ref[0], sh_ref[0])
    a = _dot(h.astype(BF16), w_ref[...]) + b_ref[...]
    g_ref[0] = a[:, :d] * _sigmoid(a[:, d:])


def _conv_pre(x, sh, sc, n1w, w, bias, tm):
    b, s, d = x.shape
    mod = pl.BlockSpec((1, 1, d), lambda bi, si: (bi, 0, 0))
    full = lambda a: pl.BlockSpec(a.shape, lambda bi, si: (0,) * a.ndim)
    tok = pl.BlockSpec((1, tm, d), lambda bi, si: (bi, si, 0))
    return pl.pallas_call(
        _conv_pre_kernel, grid=(b, s // tm),
        in_specs=[tok, mod, mod, full(n1w), full(w), full(bias)],
        out_specs=tok, out_shape=jax.ShapeDtypeStruct((b, s, d), F32),
        compiler_params=_cparams(2), name="conv_pre",
    )(x, sh, sc, n1w, w, bias)


def _conv_post_kernel(x_ref, g_ref, gp_ref, gn_ref, dw_ref, dwb_ref, lnw_ref, lnb_ref, w2_ref, b2_ref,
                      g1_ref, n2_ref, sc2_ref, sh2_ref, rwh_ref, rwl_ref, rb_ref,
                      x1_ref, h2_ref, idx_ref, gate_ref, gext_ref, *, tiles_per_batch):
    tm = x_ref.shape[0]
    si = pl.program_id(0) % tiles_per_batch
    zero = jnp.zeros_like(gp_ref[0])
    gext_ref[0:CONV_HALO] = jnp.where(si > 0, gp_ref[0], zero)
    gext_ref[CONV_HALO:CONV_HALO + tm] = g_ref[0]
    gext_ref[CONV_HALO + tm:] = jnp.where(si < tiles_per_batch - 1, gn_ref[0], zero)
    base = CONV_HALO - CONV_W // 2
    y = jnp.zeros(x_ref.shape, F32) + dwb_ref[...]
    for j in range(CONV_W):
        y = y + dw_ref[j:j + 1, :] * gext_ref[base + j:base + j + tm, :]
    mu = jnp.mean(y, axis=-1, keepdims=True)
    yc = y - mu
    var = jnp.mean(yc * yc, axis=-1, keepdims=True)
    z = _silu(yc * lax.rsqrt(var + EPS) * lnw_ref[...] + lnb_ref[...])
    mix = _dot(z.astype(BF16), w2_ref[...]) + b2_ref[...]
    _tail(x_ref[...], mix, g1_ref[0], n2_ref[...], sc2_ref[0], sh2_ref[0],
          rwh_ref, rwl_ref, rb_ref, x1_ref, h2_ref, idx_ref, gate_ref)


def _conv_post(x2d, g, dw, dwb, lnw, lnb, w2, b2, g1, n2w, sc2, sh2, rwh, rwl, rb, tm):
    t, d = x2d.shape
    b, s, _ = g.shape
    tpb = s // tm
    hb = tm // CONV_HALO
    nhb = s // CONV_HALO
    tok, mod, full2, in_tail, out_specs, out_shape = _tail_specs(t, d, tm, tpb)
    g_main = pl.BlockSpec((1, tm, d), lambda i: (i // tpb, i % tpb, 0))
    g_prev = pl.BlockSpec((1, CONV_HALO, d),
                          lambda i: (i // tpb, jnp.maximum((i % tpb) * hb - 1, 0), 0))
    g_next = pl.BlockSpec((1, CONV_HALO, d),
                          lambda i: (i // tpb, jnp.minimum((i % tpb + 1) * hb, nhb - 1), 0))
    return pl.pallas_call(
        functools.partial(_conv_post_kernel, tiles_per_batch=tpb), grid=(t // tm,),
        in_specs=[tok(d), g_main, g_prev, g_next, full2(dw.shape), full2(dwb.shape), full2(lnw.shape),
                  full2(lnb.shape), full2(w2.shape), full2(b2.shape)] + in_tail,
        out_specs=out_specs, out_shape=out_shape,
        scratch_shapes=[pltpu.VMEM((tm + 2 * CONV_HALO, d), F32)],
        compiler_params=_cparams(1), name="conv_post",
    )(x2d, g, g, g, dw, dwb, lnw, lnb, w2, b2, g1, n2w, sc2, sh2, rwh, rwl, rb)


PERMUTE_ROWS = 2048
PERMUTE_UNROLL = 8


def _permute_kernel(idx_ref, src_ref, out_ref, sem, *, scatter, n_src):
    step = pl.program_id(0)
    rows = idx_ref.shape[-1]
    base = step * rows

    def copy(j):
        i = idx_ref[0, 0, j]
        if scatter:
            s, d = lax.rem(base + j, n_src), i
        else:
            s, d = i, base + j
        return pltpu.make_async_copy(src_ref.at[pl.ds(s, 1)], out_ref.at[pl.ds(d, 1)], sem)

    def body(c, carry):
        for u in range(PERMUTE_UNROLL):
            copy(c * PERMUTE_UNROLL + u).start()
        return carry

    lax.fori_loop(0, rows // PERMUTE_UNROLL, body, 0)
    pltpu.make_async_copy(src_ref.at[pl.ds(0, rows)], out_ref.at[pl.ds(0, rows)], sem).wait()


def _permute_rows(src, idx_flat, n_out, scatter):
    n_src, d = src.shape
    n = idx_flat.shape[0]
    steps = n // PERMUTE_ROWS
    idx3 = idx_flat.reshape(steps, 1, PERMUTE_ROWS)
    return pl.pallas_call(
        functools.partial(_permute_kernel, scatter=scatter, n_src=n_src), grid=(steps,),
        in_specs=[pl.BlockSpec((1, 1, PERMUTE_ROWS), lambda i: (i, 0, 0), memory_space=pltpu.SMEM),
                  pl.BlockSpec(memory_space=pl.ANY)],
        out_specs=pl.BlockSpec(memory_space=pl.ANY),
        out_shape=jax.ShapeDtypeStruct((n_out, d), src.dtype),
        scratch_shapes=[pltpu.SemaphoreType.DMA(())],
        compiler_params=pltpu.CompilerParams(dimension_semantics=("arbitrary",)),
        name="permute_scatter" if scatter else "permute_gather",
    )(idx3, src)


def _experts_kernel(blk_e_ref, nused_ref, xs_ref, wgu_ref, wd_ref, ys_ref):
    used = pl.program_id(0) < nused_ref[0]

    @pl.when(used)
    def _():
        x = xs_ref[...].astype(BF16)
        gu = _dot(x, wgu_ref[0])
        u = _silu(gu[:, :D_EXPERT]) * gu[:, D_EXPERT:]
        ys_ref[...] = _dot(u.astype(BF16), wd_ref[0])

    @pl.when(jnp.logical_not(used))
    def _():
        ys_ref[...] = jnp.zeros_like(ys_ref)


def _experts(xs, blk_e, nused, wgu, wd):
    n, d = xs.shape
    nb = n // MOE_BLOCK
    grid_spec = pltpu.PrefetchScalarGridSpec(
        num_scalar_prefetch=2, grid=(nb,),
        in_specs=[pl.BlockSpec((MOE_BLOCK, d), lambda i, be, nu: (jnp.minimum(i, nu[0] - 1), 0)),
                  pl.BlockSpec((1, d, 2 * D_EXPERT), lambda i, be, nu: (be[i], 0, 0)),
                  pl.BlockSpec((1, D_EXPERT, d), lambda i, be, nu: (be[i], 0, 0))],
        out_specs=pl.BlockSpec((MOE_BLOCK, d), lambda i, be, nu: (i, 0)))
    return pl.pallas_call(
        _experts_kernel, grid_spec=grid_spec, out_shape=jax.ShapeDtypeStruct((n, d), F32),
        compiler_params=_cparams(1), name="moe_experts",
    )(blk_e, nused, xs, wgu, wd)


def _combine_kernel(x1_ref, h2_ref, ysu_ref, gt_ref, g2_ref, wsgu_ref, wsd_ref, o_ref):
    gt = gt_ref[...]
    y = gt[:, 0:1] * ysu_ref[0]
    for k in range(1, TOP_K):
        y = y + gt[:, k:k + 1] * ysu_ref[k]
    hb = h2_ref[...].astype(BF16)
    gu = _dot(hb, wsgu_ref[...])
    ds = wsd_ref.shape[0]
    u = _silu(gu[:, :ds]) * gu[:, ds:]
    y = y + _dot(u.astype(BF16), wsd_ref[...])
    o_ref[...] = x1_ref[...] + g2_ref[0] * y


def _combine(x1, h2, ysu, gates_tk, g2, wsgu, wsd, seq, tm):
    t, d = x1.shape
    tpb = seq // tm
    tok = pl.BlockSpec((tm, d), lambda i: (i, 0))
    full2 = lambda a: pl.BlockSpec(a.shape, lambda i: (0, 0))
    return pl.pallas_call(
        _combine_kernel, grid=(t // tm,),
        in_specs=[tok, tok, pl.BlockSpec((TOP_K, tm, d), lambda i: (0, i, 0)),
                  pl.BlockSpec((tm, TOP_K), lambda i: (i, 0)),
                  pl.BlockSpec((1, 1, d), lambda i: (i // tpb, 0, 0)), full2(wsgu), full2(wsd)],
        out_specs=tok, out_shape=jax.ShapeDtypeStruct((t, d), F32),
        compiler_params=_cparams(1), name="moe_combine",
    )(x1, h2, ysu, gates_tk, g2, wsgu, wsd)


def _moe(x1, h2, idx, gates, g2, wgu, wd, wsgu, wsd, seq, tm):
    t, d = x1.shape
    a = TOP_K * t
    n_blocks = a // MOE_BLOCK + N_EXPERTS
    onehot = (idx[:, :, None] == jnp.arange(N_EXPERTS, dtype=jnp.int32)).sum(0).astype(jnp.int32)
    before = jnp.cumsum(onehot, axis=0) - onehot
    counts = onehot.sum(0)
    padded = (counts + MOE_BLOCK - 1) // MOE_BLOCK * MOE_BLOCK
    pend = jnp.cumsum(padded)
    pstart = pend - padded
    rank = jnp.take_along_axis(before, idx.T, axis=1).T
    dest = (pstart[idx] + rank).astype(jnp.int32).reshape(a)
    blk_e = jnp.minimum(jnp.searchsorted(pend, jnp.arange(n_blocks) * MOE_BLOCK, side='right'),
                        N_EXPERTS - 1).astype(jnp.int32)
    nused = (pend[-1:] // MOE_BLOCK).astype(jnp.int32)
    n_pad = n_blocks * MOE_BLOCK - a
    pad = padded - counts
    padcum = jnp.cumsum(pad)
    j = jnp.arange(n_pad, dtype=jnp.int32)
    pe = jnp.minimum(jnp.searchsorted(padcum, j, side='right'), N_EXPERTS - 1)
    pad_slot = jnp.where(j < padcum[-1], (pstart + counts)[pe] + j - (padcum - pad)[pe],
                         pend[-1] + j - padcum[-1]).astype(jnp.int32)

    xs = _permute_rows(h2, jnp.concatenate([dest, pad_slot]), n_blocks * MOE_BLOCK, scatter=True)
    ys = _experts(xs, blk_e, nused, wgu, wd)
    ysu = _permute_rows(ys, dest, a, scatter=False).reshape(TOP_K, t, d)
    return _combine(x1, h2, ysu, gates.T, g2, wsgu, wsd, seq, tm)


def _axial_tables(seq, wq, wk):
    t = jnp.arange(seq)
    row = (t // GRID_W).astype(F32)
    col = (t % GRID_W).astype(F32)
    n_freq = MLA_ROPE // 4
    inv = ROPE_THETA ** (-jnp.arange(n_freq, dtype=F32) / n_freq)
    ang = jnp.concatenate([row[:, None] * inv, col[:, None] * inv], axis=-1)
    cos, sin = jnp.cos(ang), jnp.sin(ang)
    half = MLA_ROPE // 2
    pad = jnp.zeros((seq, HEAD_PAD - MLA_QK), F32)

    def tables(w, f):
        w0, w1, w2 = w[:MLA_NOPE], w[MLA_NOPE:MLA_NOPE + half], w[MLA_NOPE + half:]
        c_t = jnp.concatenate([jnp.broadcast_to(w0, (seq, MLA_NOPE)), w1 * cos, w2 * cos, pad], axis=-1)
        s_t = jnp.concatenate([jnp.zeros((seq, MLA_NOPE), F32), -w2 * sin, w1 * sin, pad], axis=-1)
        return c_t * f, s_t * f

    cq, sq = tables(wq, MLA_QK ** -0.5 * math.log2(math.e))
    ck, sk = tables(wk, 1.0)
    return cq, sq, ck, sk


def _attn_params(seq, w_in, qa_w, kva_w, w_q_b, w_kv_b, mq_w, mk_w, naq_w, nak_w):
    d = w_in.shape[0]
    half = MLA_ROPE // 2
    o_pe = MLA_Q_LORA + MLA_KV_LORA
    o_na = o_pe + MLA_ROPE
    z = lambda n: jnp.zeros((d, n), F32)
    pe1, pe2 = w_in[:, o_pe:o_pe + half], w_in[:, o_pe + half:o_na]
    pe = jnp.concatenate([z(MLA_NOPE), pe1, pe2, z(HEAD_PAD - MLA_QK)], axis=1)
    pes = jnp.concatenate([z(MLA_NOPE), pe2, pe1, z(HEAD_PAD - MLA_QK)], axis=1)
    w_in_r = jnp.concatenate([w_in[:, :o_pe], pe, pes, w_in[:, o_na:]], axis=1).astype(BF16)

    wq = w_q_b.reshape(MLA_Q_LORA, MLA_HEADS, MLA_QK)
    zq = lambda n: jnp.zeros((MLA_Q_LORA, MLA_HEADS, n), F32)
    wq_a = jnp.concatenate([wq, zq(HEAD_PAD - MLA_QK)], axis=-1)
    wq_b = jnp.concatenate([zq(MLA_NOPE), wq[..., MLA_NOPE + half:], wq[..., MLA_NOPE:MLA_NOPE + half],
                            zq(HEAD_PAD - MLA_QK)], axis=-1)
    wkv = w_kv_b.reshape(MLA_KV_LORA, MLA_HEADS, MLA_NOPE + MLA_V)
    zk = jnp.zeros((MLA_KV_LORA, MLA_HEADS, HEAD_PAD - MLA_NOPE), F32)
    wk = jnp.concatenate([wkv[..., :MLA_NOPE], zk], axis=-1)
    wv = jnp.concatenate([wkv[..., MLA_NOPE:], zk], axis=-1)
    hw = MLA_HEADS * HEAD_PAD
    cq, sq, ck, sk = _axial_tables(seq, mq_w, mk_w)
    return {
        "w_in": w_in_r, "qa_w": qa_w.reshape(1, -1), "kva_w": kva_w.reshape(1, -1),
        "wq_a": wq_a.reshape(MLA_Q_LORA, hw).astype(BF16), "wq_b": wq_b.reshape(MLA_Q_LORA, hw).astype(BF16),
        "wk": wk.reshape(MLA_KV_LORA, hw).astype(BF16), "wv": wv.reshape(MLA_KV_LORA, hw).astype(BF16),
        "cq": cq, "sq": sq, "ck": ck, "sk": sk,
        "na_qw": (jnp.tile(naq_w, NA_HEADS) * NA_HD ** -0.5).reshape(1, -1),
        "na_kw": jnp.tile(nak_w, NA_HEADS).reshape(1, -1),
    }


def _split_hi_lo(w):
    hi = w.astype(BF16)
    return hi, (w - hi.astype(F32)).astype(BF16)


TOKEN_TILE = 512
CONV_TILE = 256
COMBINE_TILE = 256
FLASH_TQ = 512
FLASH_TK = 512


def kernel(x, c, norm1_w, norm2_w, ada_w, ada_b, w_in, q_a_norm_w, kv_a_norm_w, w_q_b, w_kv_b,
           mla_q_norm_w, mla_k_norm_w, na_q_norm_w, na_k_norm_w, na_rpb, w_out, pw1_w, pw1_b, dw_w, dw_b,
           conv_ln_w, conv_ln_b, pw2_w, pw2_b, router_w, router_b, exp_w_gate, exp_w_up, exp_w_down,
           sh_w_gate, sh_w_up, sh_w_down):
    b, s, d = x.shape
    depth = ada_w.shape[0]
    t = b * s
    mod = _ada_mod(c, ada_w, ada_b)
    row = lambda v: v.reshape(1, -1)
    for l in range(depth):
        m6 = mod[l].reshape(b, 1, 6, d)
        sh1, sc1, g1, sh2, sc2, g2 = (m6[:, :, i] for i in range(6))
        i = l // 2
        rwh, rwl = _split_hi_lo(router_w[l].T)
        rb = router_b[l].reshape(N_EXPERTS, 1)
        tail_args = (g1, row(norm2_w[l]), sc2, sh2, rwh, rwl, rb)
        if l % 2 == 0:
            prm = _attn_params(s, w_in[i], q_a_norm_w[i], kv_a_norm_w[i], w_q_b[i], w_kv_b[i],
                               mla_q_norm_w[i], mla_k_norm_w[i], na_q_norm_w[i], na_k_norm_w[i])
            qm, km, vm, qn, kn, vn = _attn_pre(x, sh1, sc1, row(norm1_w[l]), prm, TOKEN_TILE)
            mla_o = _mla_flash(qm, km, vm, FLASH_TQ, FLASH_TK)
            na_o = _na_attention(qn, kn, vn, _na_table(na_rpb[i], s // GRID_W))
            wo = w_out[i].astype(BF16)
            nm = MLA_HEADS * MLA_V
            x1, h2, idx, gates = _post_mix(x.reshape(t, d), mla_o.reshape(t, -1), na_o.reshape(t, -1),
                                           wo[:nm], wo[nm:], *tail_args, s, TOKEN_TILE)
        else:
            g = _conv_pre(x, sh1, sc1, row(norm1_w[l]), pw1_w[i].astype(BF16), row(pw1_b[i]), TOKEN_TILE)
            x1, h2, idx, gates = _conv_post(x.reshape(t, d), g, dw_w[i], row(dw_b[i]), row(conv_ln_w[i]),
                                            row(conv_ln_b[i]), pw2_w[i].astype(BF16), row(pw2_b[i]),
                                            *tail_args, CONV_TILE)
        wgu = jnp.concatenate([exp_w_gate[l], exp_w_up[l]], axis=-1).astype(BF16)
        wsgu = jnp.concatenate([sh_w_gate[l], sh_w_up[l]], axis=-1).astype(BF16)
        x = _moe(x1, h2, idx, gates, g2, wgu, exp_w_down[l].astype(BF16), wsgu,
                 sh_w_down[l].astype(BF16), s, COMBINE_TILE).reshape(b, s, d)
    return x
```

```python
import functools
import math

import jax
import jax.numpy as jnp
from jax import lax
from jax.experimental import pallas as pl
from jax.experimental.pallas import tpu as pltpu

F32 = jnp.float32
BF16 = jnp.bfloat16

EPS = 1e-6
GRID_W = 64
LANES = 128
HEAD_PAD = 128

MLA_HEADS = 8
MLA_Q_LORA = 256
MLA_KV_LORA = 128
MLA_NOPE = 64
MLA_ROPE = 32
MLA_V = 64
MLA_QK = MLA_NOPE + MLA_ROPE
ROPE_THETA = 10000.0

NA_HEADS = 8
NA_HD = 64
NA_KH = 8
NA_KW = 16
NA_QROWS = 8
NA_KROWS = 16

CONV_W = 31
CONV_HALO = 16

N_EXPERTS = 64
TOP_K = 8
N_GROUPS = 8
TOPK_GROUPS = 4
GROUP_SIZE = N_EXPERTS // N_GROUPS
D_EXPERT = 256
ROUTED_SCALE = 2.5
MOE_BLOCK = 512

NEG_BIG = -1e30
VMEM_LIMIT = 56 * 1024 * 1024


def _cparams(n_axes):
    return pltpu.CompilerParams(dimension_semantics=("arbitrary",) * n_axes,
                                vmem_limit_bytes=VMEM_LIMIT)


def _dot(a, b):
    return jnp.dot(a, b, preferred_element_type=F32)


def _dot_nt(a, b):
    return lax.dot_general(a, b, (((1,), (1,)), ((), ())), preferred_element_type=F32)


def _rms(x):
    return x * lax.rsqrt(jnp.mean(x * x, axis=-1, keepdims=True) + EPS)


def _modnorm(x, w, sc, sh):
    return _rms(x) * w * (1.0 + sc) + sh


def _sigmoid(x):
    return 1.0 / (1.0 + jnp.exp(-x))


def _silu(x):
    return x * _sigmoid(x)


def _ada_kernel(c_ref, w_ref, b_ref, o_ref):
    c = c_ref[...]
    ca = _silu(c)
    o_ref[0] = jnp.dot(ca, w_ref[0], preferred_element_type=F32,
                       precision=lax.Precision.HIGHEST) + b_ref[0]


def _ada_mod(c, ada_w, ada_b):
    depth, d, d6 = ada_w.shape
    b = c.shape[0]
    cn = d
    return pl.pallas_call(
        _ada_kernel,
        grid=(depth, d6 // cn),
        in_specs=[pl.BlockSpec((b, d), lambda l, j: (0, 0)),
                  pl.BlockSpec((1, d, cn), lambda l, j: (l, 0, j)),
                  pl.BlockSpec((1, 1, cn), lambda l, j: (l, 0, j))],
        out_specs=pl.BlockSpec((1, b, cn), lambda l, j: (l, 0, j)),
        out_shape=jax.ShapeDtypeStruct((depth, b, d6), F32),
        compiler_params=_cparams(2),
        name="ada_mod",
    )(c, ada_w, ada_b.reshape(depth, 1, d6))


def _attn_pre_kernel(x_ref, sh_ref, sc_ref, n1_ref, win_ref, qaw_ref, kvaw_ref, wqa_ref, wqb_ref,
                     wk_ref, wv_ref, cq_ref, sq_ref, ck_ref, sk_ref, naq_ref, nak_ref,
                     qm_ref, km_ref, vm_ref, qn_ref, kn_ref, vn_ref):
    x = x_ref[0]
    h = _modnorm(x, n1_ref[...], sc_ref[0], sh_ref[0])
    p = _dot(h.astype(BF16), win_ref[...])
    o = 0
    cq = p[:, o:o + MLA_Q_LORA]; o += MLA_Q_LORA
    ckv = p[:, o:o + MLA_KV_LORA]; o += MLA_KV_LORA
    pe = p[:, o:o + HEAD_PAD]; o += HEAD_PAD
    pes = p[:, o:o + HEAD_PAD]; o += HEAD_PAD
    nw = NA_HEADS * NA_HD
    qn = p[:, o:o + nw]; o += nw
    kn = p[:, o:o + nw]; o += nw
    vn = p[:, o:o + nw]

    lane = lax.broadcasted_iota(jnp.int32, (1, HEAD_PAD), 1)

    cqn = (_rms(cq) * qaw_ref[...]).astype(BF16)
    qa = _dot(cqn, wqa_ref[...])
    qb = _dot(cqn, wqb_ref[...])
    cq_t, sq_t = cq_ref[...], sq_ref[...]
    for hh in range(MLA_HEADS):
        sl = slice(hh * HEAD_PAD, (hh + 1) * HEAD_PAD)
        a = qa[:, sl]
        r = lax.rsqrt(jnp.sum(a * a, axis=-1, keepdims=True) * (1.0 / MLA_QK) + EPS)
        qm_ref[0, :, sl] = (r * (a * cq_t + qb[:, sl] * sq_t)).astype(BF16)

    ckvn = (_rms(ckv) * kvaw_ref[...]).astype(BF16)
    kk = _dot(ckvn, wk_ref[...])
    vv = _dot(ckvn, wv_ref[...])
    ck_t, sk_t = ck_ref[...], sk_ref[...]
    pe_b = pes * sk_t
    one_col = jnp.where(lane == MLA_V, 1.0, 0.0)
    for hh in range(MLA_HEADS):
        sl = slice(hh * HEAD_PAD, (hh + 1) * HEAD_PAD)
        a = kk[:, sl] + pe
        r = lax.rsqrt(jnp.sum(a * a, axis=-1, keepdims=True) * (1.0 / MLA_QK) + EPS)
        km_ref[0, :, sl] = (r * (a * ck_t + pe_b)).astype(BF16)
        vm_ref[0, :, sl] = (vv[:, sl] + one_col).astype(BF16)

    lo = lane < NA_HD
    for src, w_ref, dst in ((qn, naq_ref, qn_ref), (kn, nak_ref, kn_ref)):
        for g in range(nw // LANES):
            sl = slice(g * LANES, (g + 1) * LANES)
            a = src[:, sl]
            sq = a * a
            tot = jnp.sum(sq, axis=-1, keepdims=True)
            s_lo = jnp.sum(jnp.where(lo, sq, 0.0), axis=-1, keepdims=True)
            r_lo = lax.rsqrt(s_lo * (1.0 / NA_HD) + EPS)
            r_hi = lax.rsqrt((tot - s_lo) * (1.0 / NA_HD) + EPS)
            dst[0, :, sl] = (a * jnp.where(lo, r_lo, r_hi) * w_ref[:, sl]).astype(BF16)
    vn_ref[0] = vn.astype(BF16)


def _attn_pre(x, sh, sc, n1w, prm, tm):
    b, s, d = x.shape
    hw = MLA_HEADS * HEAD_PAD
    nw = NA_HEADS * NA_HD
    full = lambda a: pl.BlockSpec(a.shape, lambda bi, si: (0,) * a.ndim)
    mod = pl.BlockSpec((1, 1, d), lambda bi, si: (bi, 0, 0))
    tab = pl.BlockSpec((tm, HEAD_PAD), lambda bi, si: (si, 0))
    tok = lambda w: pl.BlockSpec((1, tm, w), lambda bi, si: (bi, si, 0))
    ws = [prm[k] for k in ("w_in", "qa_w", "kva_w", "wq_a", "wq_b", "wk", "wv")]
    ins = [x, sh, sc, n1w] + ws + [prm["cq"], prm["sq"], prm["ck"], prm["sk"], prm["na_qw"], prm["na_kw"]]
    in_specs = ([tok(d), mod, mod, full(n1w)] + [full(a) for a in ws] + [tab] * 4
                + [full(prm["na_qw"]), full(prm["na_kw"])])
    out_shape = ([jax.ShapeDtypeStruct((b, s, hw), BF16)] * 3 + [jax.ShapeDtypeStruct((b, s, nw), BF16)] * 3)
    out_specs = [tok(hw)] * 3 + [tok(nw)] * 3
    return pl.pallas_call(
        _attn_pre_kernel, grid=(b, s // tm), in_specs=in_specs, out_specs=out_specs,
        out_shape=out_shape, compiler_params=_cparams(2), name="attn_pre",
    )(*ins)


def _mla_flash_kernel(q_ref, k_ref, v_ref, o_ref, *, tk):
    s_len = k_ref.shape[1]
    tq = q_ref.shape[1]
    sls = [slice(hh * HEAD_PAD, (hh + 1) * HEAD_PAD) for hh in range(2)]
    qs = [q_ref[0, :, sl] for sl in sls]

    def body(j, carry):
        off = pl.multiple_of(j * tk, tk)
        new = []
        for hh in range(2):
            m, acc = carry[hh]
            k = k_ref[0, pl.ds(off, tk), sls[hh]]
            v = v_ref[0, pl.ds(off, tk), sls[hh]]
            s = _dot_nt(qs[hh], k)
            m_new = jnp.maximum(m, jnp.max(s, axis=-1, keepdims=True))
            p = jnp.exp2(s - m_new)
            alpha = jnp.exp2(m - m_new)
            acc = alpha * acc + _dot(p.astype(BF16), v)
            new.append((m_new, acc))
        return tuple(new)

    m0 = jnp.full((tq, 1), -jnp.inf, F32)
    a0 = jnp.zeros((tq, HEAD_PAD), F32)
    res = lax.fori_loop(0, s_len // tk, body, ((m0, a0), (m0, a0)))
    outs = [acc / acc[:, MLA_V:MLA_V + 1] for _, acc in res]
    lane = lax.broadcasted_iota(jnp.int32, (1, HEAD_PAD), 1)
    shifted = pltpu.roll(outs[1], MLA_V, axis=1)
    o_ref[0] = jnp.where(lane < MLA_V, outs[0], shifted).astype(BF16)


def _mla_flash(q, k, v, tq, tk):
    b, s, hw = q.shape
    hp = MLA_HEADS // 2
    return pl.pallas_call(
        functools.partial(_mla_flash_kernel, tk=tk),
        grid=(b, hp, s // tq),
        in_specs=[pl.BlockSpec((1, tq, 2 * HEAD_PAD), lambda bi, h, qi: (bi, qi, h)),
                  pl.BlockSpec((1, s, 2 * HEAD_PAD), lambda bi, h, qi: (bi, 0, h)),
                  pl.BlockSpec((1, s, 2 * HEAD_PAD), lambda bi, h, qi: (bi, 0, h))],
        out_specs=pl.BlockSpec((1, tq, 2 * MLA_V), lambda bi, h, qi: (bi, qi, h)),
        out_shape=jax.ShapeDtypeStruct((b, s, MLA_HEADS * MLA_V), BF16),
        compiler_params=_cparams(3), name="mla_flash",
    )(q, k, v)


def _na_kernel(q_ref, k0, k1, k2, k3, v0, v1, v2, v3, t_ref, o_ref):
    q2 = q_ref[0]
    kc = jnp.concatenate([k0[0], k1[0], k2[0], k3[0]], axis=0)
    vc = jnp.concatenate([v0[0], v1[0], v2[0], v3[0]], axis=0)
    lane = lax.broadcasted_iota(jnp.int32, (1, LANES), 1)
    lo = lane < NA_HD
    outs = []
    for hh in range(2):
        qm = jnp.where(lo if hh == 0 else jnp.logical_not(lo), q2, jnp.zeros_like(q2))
        s = _dot_nt(qm, kc) + t_ref[0, hh]
        m = jnp.max(s, axis=-1, keepdims=True)
        p = jnp.exp(s - m)
        l = jnp.sum(p, axis=-1, keepdims=True)
        outs.append(_dot(p.astype(BF16), vc) / l)
    o_ref[0] = jnp.where(lo, outs[0], outs[1]).astype(BF16)


def _na_attention(q, k, v, table):
    b, s, nw = q.shape
    rows = s // GRID_W
    nblk = rows // NA_QROWS
    tq = NA_QROWS * GRID_W
    kp = 4
    tkp = NA_KROWS * GRID_W // kp
    kmax = s // tkp - kp
    hp = NA_HEADS // 2

    def kspec(j):
        return pl.BlockSpec((1, tkp, LANES),
                            lambda h, blk, bi: (bi, jnp.clip(2 * blk - 1, 0, kmax) + j, h))

    def tmap(h, blk, bi):
        cls = jnp.where(blk == 0, 0, jnp.where(blk == nblk - 1, 2, 1))
        return (cls, h, 0, 0)

    return pl.pallas_call(
        _na_kernel, grid=(hp, nblk, b),
        in_specs=[pl.BlockSpec((1, tq, LANES), lambda h, blk, bi: (bi, blk, h))]
        + [kspec(j) for j in range(kp)] * 2
        + [pl.BlockSpec((1, 2, tq, NA_KROWS * GRID_W), tmap)],
        out_specs=pl.BlockSpec((1, tq, LANES), lambda h, blk, bi: (bi, blk, h)),
        out_shape=jax.ShapeDtypeStruct((b, s, nw), BF16),
        compiler_params=_cparams(3), name="na_attn",
    )(q, k, k, k, k, v, v, v, v, table)


def _na_table(rpb, rows):
    nblk = rows // NA_QROWS
    tabs = []
    c = jnp.arange(GRID_W)
    cs = jnp.clip(c - NA_KW // 2, 0, GRID_W - NA_KW)
    vcol = (c[None, :] >= cs[:, None]) & (c[None, :] < cs[:, None] + NA_KW)
    co = jnp.clip(c[None, :] - c[:, None] + NA_KW - 1, 0, 2 * NA_KW - 2)
    for blk in (0, min(1, nblk - 1), nblk - 1):
        r = NA_QROWS * blk + jnp.arange(NA_QROWS)
        kb = min(max(NA_QROWS * blk - NA_KH // 2, 0), rows - NA_KROWS)
        krow = kb + jnp.arange(NA_KROWS)
        rs = jnp.clip(r - NA_KH // 2, 0, rows - NA_KH)
        vrow = (krow[None, :] >= rs[:, None]) & (krow[None, :] < rs[:, None] + NA_KH)
        ro = jnp.clip(krow[None, :] - r[:, None] + NA_KH - 1, 0, 2 * NA_KH - 2)
        bias = rpb[:, ro][:, :, :, co]
        bias = bias.transpose(0, 1, 3, 2, 4)
        valid = vrow[:, None, :, None] & vcol[None, :, None, :]
        tabs.append(jnp.where(valid[None], bias, NEG_BIG).reshape(
            rpb.shape[0], NA_QROWS * GRID_W, NA_KROWS * GRID_W))
    return jnp.stack(tabs).astype(F32)


def _bfly(x, op):
    for sh in (1, 2, 4):
        x = op(x, pltpu.roll(x, sh, axis=x.ndim - 2))
    return x


def _route(h2, rwh_ref, rwl_ref, rb_ref, idx_ref, gate_ref):
    tm = h2.shape[0]
    h_hi = h2.astype(BF16)
    h_lo = (h2 - h_hi.astype(F32)).astype(BF16)
    rwh = rwh_ref[...]
    logits = _dot_nt(rwh, h_hi) + _dot_nt(rwh, h_lo) + _dot_nt(rwl_ref[...], h_hi)
    scores = _sigmoid(logits)
    sel = scores + rb_ref[...]
    sub = lax.broadcasted_iota(jnp.int32, (GROUP_SIZE, tm), 0)
    sc_g = [scores[g * GROUP_SIZE:(g + 1) * GROUP_SIZE] for g in range(N_GROUPS)]
    sel_g = [sel[g * GROUP_SIZE:(g + 1) * GROUP_SIZE] for g in range(N_GROUPS)]

    gscore = []
    for x in sel_g:
        m1 = _bfly(x, jnp.maximum)
        first = _bfly(jnp.where(x == m1, sub, GROUP_SIZE), jnp.minimum)
        m2 = _bfly(jnp.where(sub == first, -jnp.inf, x), jnp.maximum)
        gscore.append(m1 + m2)

    gmask = [jnp.zeros((GROUP_SIZE, tm), jnp.bool_) for _ in range(N_GROUPS)]
    for _ in range(TOPK_GROUPS):
        best = functools.reduce(jnp.maximum, gscore)
        gidx = functools.reduce(jnp.minimum,
                                [jnp.where(gs == best, g, N_GROUPS) for g, gs in enumerate(gscore)])
        for g in range(N_GROUPS):
            hit = gidx == g
            gmask[g] = jnp.logical_or(gmask[g], hit)
            gscore[g] = jnp.where(hit, -jnp.inf, gscore[g])

    cand = [jnp.where(gmask[g], sel_g[g], -jnp.inf) for g in range(N_GROUPS)]
    eid = [sub + g * GROUP_SIZE for g in range(N_GROUPS)]
    idx_out = jnp.zeros((TOP_K, tm), jnp.int32)
    gate_out = jnp.zeros((TOP_K, tm), F32)
    gsum = jnp.zeros((GROUP_SIZE, tm), F32)
    for k in range(TOP_K):
        best = _bfly(functools.reduce(jnp.maximum, cand), jnp.maximum)
        pick = _bfly(functools.reduce(
            jnp.minimum, [jnp.where(cand[g] == best, eid[g], N_EXPERTS) for g in range(N_GROUPS)]),
            jnp.minimum)
        gv = _bfly(functools.reduce(
            jnp.add, [jnp.where(eid[g] == pick, sc_g[g], 0.0) for g in range(N_GROUPS)]), jnp.add)
        cand = [jnp.where(eid[g] == pick, -jnp.inf, cand[g]) for g in range(N_GROUPS)]
        idx_out = jnp.where(sub == k, pick, idx_out)
        gate_out = jnp.where(sub == k, gv, gate_out)
        gsum = gsum + gv
    idx_ref[...] = idx_out
    gate_ref[...] = gate_out / gsum * ROUTED_SCALE


ROW_TILE = 8


def _load_rows(ref, start, n):
    return jnp.concatenate(
        [ref[pl.ds(start + s, n, stride=ROW_TILE), :] for s in range(ROW_TILE)], axis=-1)


def _store_rows(ref, val):
    n, d = val.shape
    w = d // ROW_TILE
    for s in range(ROW_TILE):
        ref[pl.ds(s, n, stride=ROW_TILE), :] = val[:, s * w:(s + 1) * w]


def _tail(x, mix, g1, n2w, sc2, sh2, rwh_ref, rwl_ref, rb_ref, x1_ref, h2_ref, idx_ref, gate_ref):
    x1 = x + g1 * mix
    h2 = _modnorm(x1, n2w, sc2, sh2)
    x1_ref[...] = x1
    _store_rows(h2_ref, h2)
    _route(h2, rwh_ref, rwl_ref, rb_ref, idx_ref, gate_ref)


def _post_mix_kernel(x_ref, mla_ref, na_ref, wo_a_ref, wo_b_ref, g1_ref, n2_ref, sc2_ref, sh2_ref,
                     rwh_ref, rwl_ref, rb_ref, x1_ref, h2_ref, idx_ref, gate_ref):
    mix = _dot(mla_ref[...], wo_a_ref[...]) + _dot(na_ref[...], wo_b_ref[...])
    _tail(x_ref[...], mix, g1_ref[0], n2_ref[...], sc2_ref[0], sh2_ref[0],
          rwh_ref, rwl_ref, rb_ref, x1_ref, h2_ref, idx_ref, gate_ref)


def _tail_specs(t, d, tm, tiles_per_batch):
    tok = lambda w: pl.BlockSpec((tm, w), lambda i: (i, 0))
    mod = pl.BlockSpec((1, 1, d), lambda i: (i // tiles_per_batch, 0, 0))
    full2 = lambda shp: pl.BlockSpec(shp, lambda i: (0, 0))
    in_tail = [mod, full2((1, d)), mod, mod, full2((N_EXPERTS, d)), full2((N_EXPERTS, d)),
               full2((N_EXPERTS, 1))]
    out_specs = [tok(d), pl.BlockSpec((tm * ROW_TILE, d // ROW_TILE), lambda i: (i, 0)),
                 pl.BlockSpec((TOP_K, tm), lambda i: (0, i)),
                 pl.BlockSpec((TOP_K, tm), lambda i: (0, i))]
    out_shape = [jax.ShapeDtypeStruct((t, d), F32),
                 jax.ShapeDtypeStruct((t * ROW_TILE, d // ROW_TILE), F32),
                 jax.ShapeDtypeStruct((TOP_K, t), jnp.int32), jax.ShapeDtypeStruct((TOP_K, t), F32)]
    return tok, mod, full2, in_tail, out_specs, out_shape


def _post_mix(x2d, mla_o, na_o, wo_a, wo_b, g1, n2w, sc2, sh2, rwh, rwl, rb, seq, tm):
    t, d = x2d.shape
    tok, mod, full2, in_tail, out_specs, out_shape = _tail_specs(t, d, tm, seq // tm)
    return pl.pallas_call(
        _post_mix_kernel, grid=(t // tm,),
        in_specs=[tok(d), tok(mla_o.shape[1]), tok(na_o.shape[1]), full2(wo_a.shape), full2(wo_b.shape)]
        + in_tail,
        out_specs=out_specs, out_shape=out_shape, compiler_params=_cparams(1), name="post_mix",
    )(x2d, mla_o, na_o, wo_a, wo_b, g1, n2w, sc2, sh2, rwh, rwl, rb)


def _conv_pre_kernel(x_ref, sh_ref, sc_ref, n1_ref, w_ref, b_ref, g_ref):
    d = x_ref.shape[-1]
    h = _modnorm(x_ref[0], n1_ref[...], sc_ref[0], sh_ref[0])
    a = _dot(h.astype(BF16), w_ref[...]) + b_ref[...]
    g_ref[0] = a[:, :d] * _sigmoid(a[:, d:])


def _conv_pre(x, sh, sc, n1w, w, bias, tm):
    b, s, d = x.shape
    mod = pl.BlockSpec((1, 1, d), lambda bi, si: (bi, 0, 0))
    full = lambda a: pl.BlockSpec(a.shape, lambda bi, si: (0,) * a.ndim)
    tok = pl.BlockSpec((1, tm, d), lambda bi, si: (bi, si, 0))
    return pl.pallas_call(
        _conv_pre_kernel, grid=(b, s // tm),
        in_specs=[tok, mod, mod, full(n1w), full(w), full(bias)],
        out_specs=tok, out_shape=jax.ShapeDtypeStruct((b, s, d), F32),
        compiler_params=_cparams(2), name="conv_pre",
    )(x, sh, sc, n1w, w, bias)


def _conv_post_kernel(x_ref, g_ref, gp_ref, gn_ref, dw_ref, dwb_ref, lnw_ref, lnb_ref, w2_ref, b2_ref,
                      g1_ref, n2_ref, sc2_ref, sh2_ref, rwh_ref, rwl_ref, rb_ref,
                      x1_ref, h2_ref, idx_ref, gate_ref, gext_ref, *, tiles_per_batch):
    tm = x_ref.shape[0]
    si = pl.program_id(0) % tiles_per_batch
    zero = jnp.zeros_like(gp_ref[0])
    gext_ref[0:CONV_HALO] = jnp.where(si > 0, gp_ref[0], zero)
    gext_ref[CONV_HALO:CONV_HALO + tm] = g_ref[0]
    gext_ref[CONV_HALO + tm:] = jnp.where(si < tiles_per_batch - 1, gn_ref[0], zero)
    base = CONV_HALO - CONV_W // 2
    y = jnp.zeros(x_ref.shape, F32) + dwb_ref[...]
    for j in range(CONV_W):
        y = y + dw_ref[j:j + 1, :] * gext_ref[base + j:base + j + tm, :]
    mu = jnp.mean(y, axis=-1, keepdims=True)
    yc = y - mu
    var = jnp.mean(yc * yc, axis=-1, keepdims=True)
    z = _silu(yc * lax.rsqrt(var + EPS) * lnw_ref[...] + lnb_ref[...])
    mix = _dot(z.astype(BF16), w2_ref[...]) + b2_ref[...]
    _tail(x_ref[...], mix, g1_ref[0], n2_ref[...], sc2_ref[0], sh2_ref[0],
          rwh_ref, rwl_ref, rb_ref, x1_ref, h2_ref, idx_ref, gate_ref)


def _conv_post(x2d, g, dw, dwb, lnw, lnb, w2, b2, g1, n2w, sc2, sh2, rwh, rwl, rb, tm):
    t, d = x2d.shape
    b, s, _ = g.shape
    tpb = s // tm
    hb = tm // CONV_HALO
    nhb = s // CONV_HALO
    tok, mod, full2, in_tail, out_specs, out_shape = _tail_specs(t, d, tm, tpb)
    g_main = pl.BlockSpec((1, tm, d), lambda i: (i // tpb, i % tpb, 0))
    g_prev = pl.BlockSpec((1, CONV_HALO, d),
                          lambda i: (i // tpb, jnp.maximum((i % tpb) * hb - 1, 0), 0))
    g_next = pl.BlockSpec((1, CONV_HALO, d),
                          lambda i: (i // tpb, jnp.minimum((i % tpb + 1) * hb, nhb - 1), 0))
    return pl.pallas_call(
        functools.partial(_conv_post_kernel, tiles_per_batch=tpb), grid=(t // tm,),
        in_specs=[tok(d), g_main, g_prev, g_next, full2(dw.shape), full2(dwb.shape), full2(lnw.shape),
                  full2(lnb.shape), full2(w2.shape), full2(b2.shape)] + in_tail,
        out_specs=out_specs, out_shape=out_shape,
        scratch_shapes=[pltpu.VMEM((tm + 2 * CONV_HALO, d), F32)],
        compiler_params=_cparams(1), name="conv_post",
    )(x2d, g, g, g, dw, dwb, lnw, lnb, w2, b2, g1, n2w, sc2, sh2, rwh, rwl, rb)


DISPATCH_TOKENS = 256


def _dispatch_kernel(dest_ref, h2_ref, xs_ref, sem):
    n_tok = h2_ref.shape[0] // ROW_TILE

    def copy(t, k):
        d = dest_ref[0, 0, t * TOP_K + k]
        return pltpu.make_async_copy(h2_ref.at[pl.ds(pl.multiple_of(t * ROW_TILE, ROW_TILE), ROW_TILE)],
                                     xs_ref.at[pl.ds(pl.multiple_of(d * ROW_TILE, ROW_TILE), ROW_TILE)],
                                     sem)

    def body(t, carry):
        for k in range(TOP_K):
            copy(t, k).start()
        return carry

    lax.fori_loop(0, n_tok, body, 0)
    n_rows = n_tok * TOP_K * ROW_TILE
    pltpu.make_async_copy(xs_ref.at[pl.ds(0, n_rows)], xs_ref.at[pl.ds(0, n_rows)], sem).wait()


def _dispatch(h2r, dest_tk, n_slots):
    w = h2r.shape[1]
    tok_steps = h2r.shape[0] // (DISPATCH_TOKENS * ROW_TILE)
    steps = dest_tk.shape[0]
    return pl.pallas_call(
        _dispatch_kernel, grid=(steps,),
        in_specs=[pl.BlockSpec((1, 1, DISPATCH_TOKENS * TOP_K), lambda i: (i, 0, 0),
                               memory_space=pltpu.SMEM),
                  pl.BlockSpec((DISPATCH_TOKENS * ROW_TILE, w),
                               lambda i: (jnp.minimum(i, tok_steps - 1), 0))],
        out_specs=pl.BlockSpec(memory_space=pl.ANY),
        out_shape=jax.ShapeDtypeStruct((n_slots * ROW_TILE, w), F32),
        scratch_shapes=[pltpu.SemaphoreType.DMA(())],
        compiler_params=_cparams(1), name="moe_dispatch",
    )(dest_tk, h2r)


def _experts_kernel(blk_e_ref, nused_ref, xs_ref, wgu_ref, wd_ref, ys_ref):
    used = pl.program_id(0) < nused_ref[0]

    @pl.when(used)
    def _():
        x = _load_rows(xs_ref, 0, MOE_BLOCK).astype(BF16)
        gu = _dot(x, wgu_ref[0])
        u = _silu(gu[:, :D_EXPERT]) * gu[:, D_EXPERT:]
        _store_rows(ys_ref, _dot(u.astype(BF16), wd_ref[0]))

    @pl.when(jnp.logical_not(used))
    def _():
        ys_ref[...] = jnp.zeros_like(ys_ref)


def _experts(xs, blk_e, nused, wgu, wd):
    n, w = xs.shape
    d = w * ROW_TILE
    rows = MOE_BLOCK * ROW_TILE
    nb = n // rows
    grid_spec = pltpu.PrefetchScalarGridSpec(
        num_scalar_prefetch=2, grid=(nb,),
        in_specs=[pl.BlockSpec((rows, w), lambda i, be, nu: (jnp.minimum(i, nu[0] - 1), 0)),
                  pl.BlockSpec((1, d, 2 * D_EXPERT), lambda i, be, nu: (be[i], 0, 0)),
                  pl.BlockSpec((1, D_EXPERT, d), lambda i, be, nu: (be[i], 0, 0))],
        out_specs=pl.BlockSpec((rows, w), lambda i, be, nu: (i, 0)))
    return pl.pallas_call(
        _experts_kernel, grid_spec=grid_spec, out_shape=jax.ShapeDtypeStruct((n, w), F32),
        compiler_params=_cparams(1), name="moe_experts",
    )(blk_e, nused, xs, wgu, wd)


def _combine_kernel(dcur_ref, dnext_ref, x1_ref, h2_ref, gt_ref, g2_ref, wsgu_ref, wsd_ref, ys_ref,
                    o_ref, ybuf, sem):
    i = pl.program_id(0)
    n = pl.num_programs(0)
    tm = x1_ref.shape[0]
    n_copies = TOP_K * tm
    slot = lax.rem(i, 2)

    def issue(d_ref, to_slot):
        def body(c, carry):
            for u in range(TOP_K):
                j = c * TOP_K + u
                d = d_ref[0, 0, j]
                pltpu.make_async_copy(
                    ys_ref.at[pl.ds(pl.multiple_of(d * ROW_TILE, ROW_TILE), ROW_TILE)],
                    ybuf.at[to_slot, pl.ds(pl.multiple_of(j * ROW_TILE, ROW_TILE), ROW_TILE)],
                    sem.at[to_slot]).start()
            return carry
        lax.fori_loop(0, n_copies // TOP_K, body, 0)

    @pl.when(i == 0)
    def _():
        issue(dcur_ref, 0)

    @pl.when(i + 1 < n)
    def _():
        issue(dnext_ref, 1 - slot)

    pltpu.make_async_copy(ys_ref.at[pl.ds(0, n_copies * ROW_TILE)], ybuf.at[slot], sem.at[slot]).wait()

    gt = gt_ref[...]
    ycur = ybuf.at[slot]
    y = gt[:, 0:1] * _load_rows(ycur, 0, tm)
    for k in range(1, TOP_K):
        y = y + gt[:, k:k + 1] * _load_rows(ycur, k * tm * ROW_TILE, tm)
    hb = _load_rows(h2_ref, 0, tm).astype(BF16)
    gu = _dot(hb, wsgu_ref[...])
    ds = wsd_ref.shape[0]
    u = _silu(gu[:, :ds]) * gu[:, ds:]
    y = y + _dot(u.astype(BF16), wsd_ref[...])
    o_ref[...] = x1_ref[...] + g2_ref[0] * y


def _combine(x1, h2r, ys, dest_kt, gates_tk, g2, wsgu, wsd, seq, tm):
    t, d = x1.shape
    w = d // ROW_TILE
    tpb = seq // tm
    steps = t // tm
    tok = pl.BlockSpec((tm, d), lambda i: (i, 0))
    full2 = lambda a: pl.BlockSpec(a.shape, lambda i: (0, 0))
    didx = lambda f: pl.BlockSpec((1, 1, TOP_K * tm), f, memory_space=pltpu.SMEM)
    return pl.pallas_call(
        _combine_kernel, grid=(steps,),
        in_specs=[didx(lambda i: (i, 0, 0)), didx(lambda i: (jnp.minimum(i + 1, steps - 1), 0, 0)),
                  tok, pl.BlockSpec((tm * ROW_TILE, w), lambda i: (i, 0)),
                  pl.BlockSpec((tm, TOP_K), lambda i: (i, 0)),
                  pl.BlockSpec((1, 1, d), lambda i: (i // tpb, 0, 0)), full2(wsgu), full2(wsd),
                  pl.BlockSpec(memory_space=pl.ANY)],
        out_specs=tok, out_shape=jax.ShapeDtypeStruct((t, d), F32),
        scratch_shapes=[pltpu.VMEM((2, TOP_K * tm * ROW_TILE, w), F32), pltpu.SemaphoreType.DMA((2,))],
        compiler_params=_cparams(1), name="moe_combine",
    )(dest_kt, dest_kt, x1, h2r, gates_tk, g2, wsgu, wsd, ys)


def _moe(x1, h2r, idx, gates, g2, wgu, wd, wsgu, wsd, seq, tm):
    t, d = x1.shape
    a = TOP_K * t
    n_blocks = a // MOE_BLOCK + N_EXPERTS
    n_slots = n_blocks * MOE_BLOCK
    eq = idx[:, :, None] == jnp.arange(N_EXPERTS, dtype=jnp.int32)
    onehot = eq.sum(0).astype(jnp.int32)
    before = jnp.cumsum(onehot, axis=0) - onehot
    counts = onehot.sum(0)
    padded = (counts + MOE_BLOCK - 1) // MOE_BLOCK * MOE_BLOCK
    pend = jnp.cumsum(padded)
    pstart = pend - padded
    dest = jnp.where(eq, (before + pstart)[None], 0).sum(-1).astype(jnp.int32)
    blk_pos = jnp.arange(n_blocks, dtype=jnp.int32) * MOE_BLOCK
    blk_e = jnp.minimum((pend[None, :] <= blk_pos[:, None]).sum(-1), N_EXPERTS - 1).astype(jnp.int32)
    nused = (pend[-1:] // MOE_BLOCK).astype(jnp.int32)
    n_pad = n_slots - a
    pad = padded - counts
    padcum = jnp.cumsum(pad)
    j = jnp.arange(n_pad, dtype=jnp.int32)
    in_e = (padcum[None, :] > j[:, None]) & ((padcum - pad)[None, :] <= j[:, None])
    slot_e = jnp.where(in_e, (pstart + counts - (padcum - pad))[None, :] + j[:, None], 0).sum(-1)
    pad_slot = jnp.where(j < padcum[-1], slot_e, pend[-1] + j - padcum[-1]).astype(jnp.int32)

    per_step = DISPATCH_TOKENS * TOP_K
    dest_tk = jnp.concatenate([dest.T.reshape(a), pad_slot]).reshape(-1, 1, per_step)
    xs = _dispatch(h2r, dest_tk, n_slots)
    ys = _experts(xs, blk_e, nused, wgu, wd)
    dest_kt = dest.reshape(TOP_K, t // tm, tm).transpose(1, 0, 2).reshape(t // tm, 1, TOP_K * tm)
    return _combine(x1, h2r, ys, dest_kt, gates.T, g2, wsgu, wsd, seq, tm)


def _axial_tables(seq, wq, wk):
    t = jnp.arange(seq)
    row = (t // GRID_W).astype(F32)
    col = (t % GRID_W).astype(F32)
    n_freq = MLA_ROPE // 4
    inv = ROPE_THETA ** (-jnp.arange(n_freq, dtype=F32) / n_freq)
    ang = jnp.concatenate([row[:, None] * inv, col[:, None] * inv], axis=-1)
    cos, sin = jnp.cos(ang), jnp.sin(ang)
    half = MLA_ROPE // 2
    pad = jnp.zeros((seq, HEAD_PAD - MLA_QK), F32)

    def tables(w, f):
        w0, w1, w2 = w[:MLA_NOPE], w[MLA_NOPE:MLA_NOPE + half], w[MLA_NOPE + half:]
        c_t = jnp.concatenate([jnp.broadcast_to(w0, (seq, MLA_NOPE)), w1 * cos, w2 * cos, pad], axis=-1)
        s_t = jnp.concatenate([jnp.zeros((seq, MLA_NOPE), F32), -w2 * sin, w1 * sin, pad], axis=-1)
        return c_t * f, s_t * f

    cq, sq = tables(wq, MLA_QK ** -0.5 * math.log2(math.e))
    ck, sk = tables(wk, 1.0)
    return cq, sq, ck, sk


def _attn_params(seq, w_in, qa_w, kva_w, w_q_b, w_kv_b, mq_w, mk_w, naq_w, nak_w):
    d = w_in.shape[0]
    half = MLA_ROPE // 2
    o_pe = MLA_Q_LORA + MLA_KV_LORA
    o_na = o_pe + MLA_ROPE
    z = lambda n: jnp.zeros((d, n), F32)
    pe1, pe2 = w_in[:, o_pe:o_pe + half], w_in[:, o_pe + half:o_na]
    pe = jnp.concatenate([z(MLA_NOPE), pe1, pe2, z(HEAD_PAD - MLA_QK)], axis=1)
    pes = jnp.concatenate([z(MLA_NOPE), pe2, pe1, z(HEAD_PAD - MLA_QK)], axis=1)
    w_in_r = jnp.concatenate([w_in[:, :o_pe], pe, pes, w_in[:, o_na:]], axis=1).astype(BF16)

    wq = w_q_b.reshape(MLA_Q_LORA, MLA_HEADS, MLA_QK)
    zq = lambda n: jnp.zeros((MLA_Q_LORA, MLA_HEADS, n), F32)
    wq_a = jnp.concatenate([wq, zq(HEAD_PAD - MLA_QK)], axis=-1)
    wq_b = jnp.concatenate([zq(MLA_NOPE), wq[..., MLA_NOPE + half:], wq[..., MLA_NOPE:MLA_NOPE + half],
                            zq(HEAD_PAD - MLA_QK)], axis=-1)
    wkv = w_kv_b.reshape(MLA_KV_LORA, MLA_HEADS, MLA_NOPE + MLA_V)
    zk = jnp.zeros((MLA_KV_LORA, MLA_HEADS, HEAD_PAD - MLA_NOPE), F32)
    wk = jnp.concatenate([wkv[..., :MLA_NOPE], zk], axis=-1)
    wv = jnp.concatenate([wkv[..., MLA_NOPE:], zk], axis=-1)
    hw = MLA_HEADS * HEAD_PAD
    cq, sq, ck, sk = _axial_tables(seq, mq_w, mk_w)
    return {
        "w_in": w_in_r, "qa_w": qa_w.reshape(1, -1), "kva_w": kva_w.reshape(1, -1),
        "wq_a": wq_a.reshape(MLA_Q_LORA, hw).astype(BF16), "wq_b": wq_b.reshape(MLA_Q_LORA, hw).astype(BF16),
        "wk": wk.reshape(MLA_KV_LORA, hw).astype(BF16), "wv": wv.reshape(MLA_KV_LORA, hw).astype(BF16),
        "cq": cq, "sq": sq, "ck": ck, "sk": sk,
        "na_qw": (jnp.tile(naq_w, NA_HEADS) * NA_HD ** -0.5).reshape(1, -1),
        "na_kw": jnp.tile(nak_w, NA_HEADS).reshape(1, -1),
    }


def _split_hi_lo(w):
    hi = w.astype(BF16)
    return hi, (w - hi.astype(F32)).astype(BF16)


TOKEN_TILE = 512
CONV_TILE = 256
COMBINE_TILE = 256
FLASH_TQ = 512
FLASH_TK = 2048


def kernel(x, c, norm1_w, norm2_w, ada_w, ada_b, w_in, q_a_norm_w, kv_a_norm_w, w_q_b, w_kv_b,
           mla_q_norm_w, mla_k_norm_w, na_q_norm_w, na_k_norm_w, na_rpb, w_out, pw1_w, pw1_b, dw_w, dw_b,
           conv_ln_w, conv_ln_b, pw2_w, pw2_b, router_w, router_b, exp_w_gate, exp_w_up, exp_w_down,
           sh_w_gate, sh_w_up, sh_w_down):
    b, s, d = x.shape
    depth = ada_w.shape[0]
    t = b * s
    mod = _ada_mod(c, ada_w, ada_b)
    row = lambda v: v.reshape(1, -1)
    for l in range(depth):
        m6 = mod[l].reshape(b, 1, 6, d)
        sh1, sc1, g1, sh2, sc2, g2 = (m6[:, :, i] for i in range(6))
        i = l // 2
        rwh, rwl = _split_hi_lo(router_w[l].T)
        rb = router_b[l].reshape(N_EXPERTS, 1)
        tail_args = (g1, row(norm2_w[l]), sc2, sh2, rwh, rwl, rb)
        if l % 2 == 0:
            prm = _attn_params(s, w_in[i], q_a_norm_w[i], kv_a_norm_w[i], w_q_b[i], w_kv_b[i],
                               mla_q_norm_w[i], mla_k_norm_w[i], na_q_norm_w[i], na_k_norm_w[i])
            qm, km, vm, qn, kn, vn = _attn_pre(x, sh1, sc1, row(norm1_w[l]), prm, TOKEN_TILE)
            mla_o = _mla_flash(qm, km, vm, FLASH_TQ, FLASH_TK)
            na_o = _na_attention(qn, kn, vn, _na_table(na_rpb[i], s // GRID_W))
            wo = w_out[i].astype(BF16)
            nm = MLA_HEADS * MLA_V
            x1, h2, idx, gates = _post_mix(x.reshape(t, d), mla_o.reshape(t, -1), na_o.reshape(t, -1),
                                           wo[:nm], wo[nm:], *tail_args, s, TOKEN_TILE)
        else:
            g = _conv_pre(x, sh1, sc1, row(norm1_w[l]), pw1_w[i].astype(BF16), row(pw1_b[i]), TOKEN_TILE)
            x1, h2, idx, gates = _conv_post(x.reshape(t, d), g, dw_w[i], row(dw_b[i]), row(conv_ln_w[i]),
                                            row(conv_ln_b[i]), pw2_w[i].astype(BF16), row(pw2_b[i]),
                                            *tail_args, CONV_TILE)
        wgu = jnp.concatenate([exp_w_gate[l], exp_w_up[l]], axis=-1).astype(BF16)
        wsgu = jnp.concatenate([sh_w_gate[l], sh_w_up[l]], axis=-1).astype(BF16)
        x = _moe(x1, h2, idx, gates, g2, wgu, exp_w_down[l].astype(BF16), wsgu,
                 sh_w_down[l].astype(BF16), s, COMBINE_TILE).reshape(b, s, d)
    return x
```

```python
import functools
import math

import jax
import jax.numpy as jnp
from jax import lax
from jax.experimental import pallas as pl
from jax.experimental.pallas import tpu as pltpu

F32 = jnp.float32
BF16 = jnp.bfloat16

EPS = 1e-6
GRID_W = 64
LANES = 128
HEAD_PAD = 128

MLA_HEADS = 8
MLA_Q_LORA = 256
MLA_KV_LORA = 128
MLA_NOPE = 64
MLA_ROPE = 32
MLA_V = 64
MLA_QK = MLA_NOPE + MLA_ROPE
ROPE_THETA = 10000.0

NA_HEADS = 8
NA_HD = 64
NA_KH = 8
NA_KW = 16
NA_QROWS = 8
NA_KROWS = 16

CONV_W = 31
CONV_HALO = 16
CONV_CHUNK = 32
SUBLANES = 8

N_EXPERTS = 64
TOP_K = 8
N_GROUPS = 8
TOPK_GROUPS = 4
GROUP_SIZE = N_EXPERTS // N_GROUPS
D_EXPERT = 256
ROUTED_SCALE = 2.5
MOE_BLOCK = 512

NEG_BIG = -1e30
VMEM_LIMIT = 56 * 1024 * 1024


def _cparams(n_axes):
    return pltpu.CompilerParams(dimension_semantics=("arbitrary",) * n_axes,
                                vmem_limit_bytes=VMEM_LIMIT)


def _dot(a, b):
    return jnp.dot(a, b, preferred_element_type=F32)


def _dot_nt(a, b):
    return lax.dot_general(a, b, (((1,), (1,)), ((), ())), preferred_element_type=F32)


def _rms(x):
    return x * lax.rsqrt(jnp.mean(x * x, axis=-1, keepdims=True) + EPS)


def _modnorm(x, w, sc, sh):
    return _rms(x) * w * (1.0 + sc) + sh


def _sigmoid(x):
    return 1.0 / (1.0 + jnp.exp(-x))


def _silu(x):
    return x * _sigmoid(x)


def _ada_kernel(c_ref, w_ref, b_ref, o_ref):
    c = c_ref[...]
    ca = _silu(c)
    o_ref[0] = jnp.dot(ca, w_ref[0], preferred_element_type=F32,
                       precision=lax.Precision.HIGHEST) + b_ref[0]


def _ada_mod(c, ada_w, ada_b):
    depth, d, d6 = ada_w.shape
    b = c.shape[0]
    cn = d
    return pl.pallas_call(
        _ada_kernel,
        grid=(depth, d6 // cn),
        in_specs=[pl.BlockSpec((b, d), lambda l, j: (0, 0)),
                  pl.BlockSpec((1, d, cn), lambda l, j: (l, 0, j)),
                  pl.BlockSpec((1, 1, cn), lambda l, j: (l, 0, j))],
        out_specs=pl.BlockSpec((1, b, cn), lambda l, j: (l, 0, j)),
        out_shape=jax.ShapeDtypeStruct((depth, b, d6), F32),
        compiler_params=_cparams(2),
        name="ada_mod",
    )(c, ada_w, ada_b.reshape(depth, 1, d6))


def _attn_pre_kernel(x_ref, sh_ref, sc_ref, n1_ref, win_ref, qaw_ref, kvaw_ref, wqa_ref, wqb_ref,
                     wk_ref, wv_ref, cq_ref, sq_ref, ck_ref, sk_ref, naq_ref, nak_ref,
                     qm_ref, km_ref, vm_ref, qn_ref, kn_ref, vn_ref):
    x = x_ref[0]
    h = _modnorm(x, n1_ref[...], sc_ref[0], sh_ref[0])
    p = _dot(h.astype(BF16), win_ref[...])
    o = 0
    cq = p[:, o:o + MLA_Q_LORA]; o += MLA_Q_LORA
    ckv = p[:, o:o + MLA_KV_LORA]; o += MLA_KV_LORA
    pe = p[:, o:o + HEAD_PAD]; o += HEAD_PAD
    pes = p[:, o:o + HEAD_PAD]; o += HEAD_PAD
    nw = NA_HEADS * NA_HD
    qn = p[:, o:o + nw]; o += nw
    kn = p[:, o:o + nw]; o += nw
    vn = p[:, o:o + nw]

    lane = lax.broadcasted_iota(jnp.int32, (1, HEAD_PAD), 1)

    cqn = (_rms(cq) * qaw_ref[...]).astype(BF16)
    qa = _dot(cqn, wqa_ref[...])
    qb = _dot(cqn, wqb_ref[...])
    cq_t, sq_t = cq_ref[...], sq_ref[...]
    for hh in range(MLA_HEADS):
        sl = slice(hh * HEAD_PAD, (hh + 1) * HEAD_PAD)
        a = qa[:, sl]
        r = lax.rsqrt(jnp.sum(a * a, axis=-1, keepdims=True) * (1.0 / MLA_QK) + EPS)
        qm_ref[0, :, sl] = (r * (a * cq_t + qb[:, sl] * sq_t)).astype(BF16)

    ckvn = (_rms(ckv) * kvaw_ref[...]).astype(BF16)
    kk = _dot(ckvn, wk_ref[...])
    vv = _dot(ckvn, wv_ref[...])
    ck_t, sk_t = ck_ref[...], sk_ref[...]
    pe_b = pes * sk_t
    one_col = jnp.where(lane == MLA_V, 1.0, 0.0)
    for hh in range(MLA_HEADS):
        sl = slice(hh * HEAD_PAD, (hh + 1) * HEAD_PAD)
        a = kk[:, sl] + pe
        r = lax.rsqrt(jnp.sum(a * a, axis=-1, keepdims=True) * (1.0 / MLA_QK) + EPS)
        km_ref[0, :, sl] = (r * (a * ck_t + pe_b)).astype(BF16)
        vm_ref[0, :, sl] = (vv[:, sl] + one_col).astype(BF16)

    lo = lane < NA_HD
    for src, w_ref, dst in ((qn, naq_ref, qn_ref), (kn, nak_ref, kn_ref)):
        for g in range(nw // LANES):
            sl = slice(g * LANES, (g + 1) * LANES)
            a = src[:, sl]
            sq = a * a
            tot = jnp.sum(sq, axis=-1, keepdims=True)
            s_lo = jnp.sum(jnp.where(lo, sq, 0.0), axis=-1, keepdims=True)
            r_lo = lax.rsqrt(s_lo * (1.0 / NA_HD) + EPS)
            r_hi = lax.rsqrt((tot - s_lo) * (1.0 / NA_HD) + EPS)
            dst[0, :, sl] = (a * jnp.where(lo, r_lo, r_hi) * w_ref[:, sl]).astype(BF16)
    vn_ref[0] = vn.astype(BF16)


def _attn_pre(x, sh, sc, n1w, prm, tm):
    b, s, d = x.shape
    hw = MLA_HEADS * HEAD_PAD
    nw = NA_HEADS * NA_HD
    full = lambda a: pl.BlockSpec(a.shape, lambda bi, si: (0,) * a.ndim)
    mod = pl.BlockSpec((1, 1, d), lambda bi, si: (bi, 0, 0))
    tab = pl.BlockSpec((tm, HEAD_PAD), lambda bi, si: (si, 0))
    tok = lambda w: pl.BlockSpec((1, tm, w), lambda bi, si: (bi, si, 0))
    ws = [prm[k] for k in ("w_in", "qa_w", "kva_w", "wq_a", "wq_b", "wk", "wv")]
    ins = [x, sh, sc, n1w] + ws + [prm["cq"], prm["sq"], prm["ck"], prm["sk"], prm["na_qw"], prm["na_kw"]]
    in_specs = ([tok(d), mod, mod, full(n1w)] + [full(a) for a in ws] + [tab] * 4
                + [full(prm["na_qw"]), full(prm["na_kw"])])
    out_shape = ([jax.ShapeDtypeStruct((b, s, hw), BF16)] * 3 + [jax.ShapeDtypeStruct((b, s, nw), BF16)] * 3)
    out_specs = [tok(hw)] * 3 + [tok(nw)] * 3
    return pl.pallas_call(
        _attn_pre_kernel, grid=(b, s // tm), in_specs=in_specs, out_specs=out_specs,
        out_shape=out_shape, compiler_params=_cparams(2), name="attn_pre",
    )(*ins)


def _mla_flash_kernel(q_ref, k_ref, v_ref, o_ref, *, tk):
    s_len = k_ref.shape[1]
    tq = q_ref.shape[1]
    sls = [slice(hh * HEAD_PAD, (hh + 1) * HEAD_PAD) for hh in range(2)]
    qs = [q_ref[0, :, sl] for sl in sls]

    def body(j, carry):
        off = pl.multiple_of(j * tk, tk)
        new = []
        for hh in range(2):
            m, acc = carry[hh]
            k = k_ref[0, pl.ds(off, tk), sls[hh]]
            v = v_ref[0, pl.ds(off, tk), sls[hh]]
            s = _dot_nt(qs[hh], k)
            m_new = jnp.maximum(m, jnp.max(s, axis=-1, keepdims=True))
            p = jnp.exp2(s - m_new)
            alpha = jnp.exp2(m - m_new)
            acc = alpha * acc + _dot(p.astype(BF16), v)
            new.append((m_new, acc))
        return tuple(new)

    m0 = jnp.full((tq, 1), -jnp.inf, F32)
    a0 = jnp.zeros((tq, HEAD_PAD), F32)
    res = lax.fori_loop(0, s_len // tk, body, ((m0, a0), (m0, a0)))
    outs = [acc / acc[:, MLA_V:MLA_V + 1] for _, acc in res]
    lane = lax.broadcasted_iota(jnp.int32, (1, HEAD_PAD), 1)
    shifted = pltpu.roll(outs[1], MLA_V, axis=1)
    o_ref[0] = jnp.where(lane < MLA_V, outs[0], shifted).astype(BF16)


def _mla_flash(q, k, v, tq, tk):
    b, s, hw = q.shape
    hp = MLA_HEADS // 2
    return pl.pallas_call(
        functools.partial(_mla_flash_kernel, tk=tk),
        grid=(b, hp, s // tq),
        in_specs=[pl.BlockSpec((1, tq, 2 * HEAD_PAD), lambda bi, h, qi: (bi, qi, h)),
                  pl.BlockSpec((1, s, 2 * HEAD_PAD), lambda bi, h, qi: (bi, 0, h)),
                  pl.BlockSpec((1, s, 2 * HEAD_PAD), lambda bi, h, qi: (bi, 0, h))],
        out_specs=pl.BlockSpec((1, tq, 2 * MLA_V), lambda bi, h, qi: (bi, qi, h)),
        out_shape=jax.ShapeDtypeStruct((b, s, MLA_HEADS * MLA_V), BF16),
        compiler_params=_cparams(3), name="mla_flash",
    )(q, k, v)


def _na_kernel(q_ref, k0, k1, k2, k3, v0, v1, v2, v3, t_ref, o_ref):
    q2 = q_ref[0]
    kc = jnp.concatenate([k0[0], k1[0], k2[0], k3[0]], axis=0)
    vc = jnp.concatenate([v0[0], v1[0], v2[0], v3[0]], axis=0)
    lane = lax.broadcasted_iota(jnp.int32, (1, LANES), 1)
    lo = lane < NA_HD
    outs = []
    for hh in range(2):
        qm = jnp.where(lo if hh == 0 else jnp.logical_not(lo), q2, jnp.zeros_like(q2))
        s = _dot_nt(qm, kc) + t_ref[0, hh]
        m = jnp.max(s, axis=-1, keepdims=True)
        p = jnp.exp(s - m)
        l = jnp.sum(p, axis=-1, keepdims=True)
        outs.append(_dot(p.astype(BF16), vc) / l)
    o_ref[0] = jnp.where(lo, outs[0], outs[1]).astype(BF16)


def _na_attention(q, k, v, table):
    b, s, nw = q.shape
    rows = s // GRID_W
    nblk = rows // NA_QROWS
    tq = NA_QROWS * GRID_W
    kp = 4
    tkp = NA_KROWS * GRID_W // kp
    kmax = s // tkp - kp
    hp = NA_HEADS // 2

    def kspec(j):
        return pl.BlockSpec((1, tkp, LANES),
                            lambda h, blk, bi: (bi, jnp.clip(2 * blk - 1, 0, kmax) + j, h))

    def tmap(h, blk, bi):
        cls = jnp.where(blk == 0, 0, jnp.where(blk == nblk - 1, 2, 1))
        return (cls, h, 0, 0)

    return pl.pallas_call(
        _na_kernel, grid=(hp, nblk, b),
        in_specs=[pl.BlockSpec((1, tq, LANES), lambda h, blk, bi: (bi, blk, h))]
        + [kspec(j) for j in range(kp)] * 2
        + [pl.BlockSpec((1, 2, tq, NA_KROWS * GRID_W), tmap)],
        out_specs=pl.BlockSpec((1, tq, LANES), lambda h, blk, bi: (bi, blk, h)),
        out_shape=jax.ShapeDtypeStruct((b, s, nw), BF16),
        compiler_params=_cparams(3), name="na_attn",
    )(q, k, k, k, k, v, v, v, v, table)


def _na_table(rpb, rows):
    nblk = rows // NA_QROWS
    tabs = []
    c = jnp.arange(GRID_W)
    cs = jnp.clip(c - NA_KW // 2, 0, GRID_W - NA_KW)
    vcol = (c[None, :] >= cs[:, None]) & (c[None, :] < cs[:, None] + NA_KW)
    co = jnp.clip(c[None, :] - c[:, None] + NA_KW - 1, 0, 2 * NA_KW - 2)
    for blk in (0, min(1, nblk - 1), nblk - 1):
        r = NA_QROWS * blk + jnp.arange(NA_QROWS)
        kb = min(max(NA_QROWS * blk - NA_KH // 2, 0), rows - NA_KROWS)
        krow = kb + jnp.arange(NA_KROWS)
        rs = jnp.clip(r - NA_KH // 2, 0, rows - NA_KH)
        vrow = (krow[None, :] >= rs[:, None]) & (krow[None, :] < rs[:, None] + NA_KH)
        ro = jnp.clip(krow[None, :] - r[:, None] + NA_KH - 1, 0, 2 * NA_KH - 2)
        bias = rpb[:, ro][:, :, :, co]
        bias = bias.transpose(0, 1, 3, 2, 4)
        valid = vrow[:, None, :, None] & vcol[None, :, None, :]
        tabs.append(jnp.where(valid[None], bias, NEG_BIG).reshape(
            rpb.shape[0], NA_QROWS * GRID_W, NA_KROWS * GRID_W))
    return jnp.stack(tabs).astype(F32)


def _bfly(x, op):
    for sh in (1, 2, 4):
        x = op(x, pltpu.roll(x, sh, axis=x.ndim - 2))
    return x


def _route(h2, rwh_ref, rwl_ref, rb_ref, idx_ref, gate_ref):
    tm = h2.shape[0]
    h_hi = h2.astype(BF16)
    h_lo = (h2 - h_hi.astype(F32)).astype(BF16)
    rwh = rwh_ref[...]
    logits = _dot_nt(rwh, h_hi) + _dot_nt(rwh, h_lo) + _dot_nt(rwl_ref[...], h_hi)
    scores = _sigmoid(logits)
    sel = scores + rb_ref[...]
    sub = lax.broadcasted_iota(jnp.int32, (GROUP_SIZE, tm), 0)
    sc_g = [scores[g * GROUP_SIZE:(g + 1) * GROUP_SIZE] for g in range(N_GROUPS)]
    sel_g = [sel[g * GROUP_SIZE:(g + 1) * GROUP_SIZE] for g in range(N_GROUPS)]

    gscore = []
    for x in sel_g:
        m1 = _bfly(x, jnp.maximum)
        first = _bfly(jnp.where(x == m1, sub, GROUP_SIZE), jnp.minimum)
        m2 = _bfly(jnp.where(sub == first, -jnp.inf, x), jnp.maximum)
        gscore.append(m1 + m2)

    gmask = [jnp.zeros((GROUP_SIZE, tm), jnp.bool_) for _ in range(N_GROUPS)]
    for _ in range(TOPK_GROUPS):
        best = functools.reduce(jnp.maximum, gscore)
        gidx = functools.reduce(jnp.minimum,
                                [jnp.where(gs == best, g, N_GROUPS) for g, gs in enumerate(gscore)])
        for g in range(N_GROUPS):
            hit = gidx == g
            gmask[g] = jnp.logical_or(gmask[g], hit)
            gscore[g] = jnp.where(hit, -jnp.inf, gscore[g])

    cand = [jnp.where(gmask[g], sel_g[g], -jnp.inf) for g in range(N_GROUPS)]
    eid = [sub + g * GROUP_SIZE for g in range(N_GROUPS)]
    idx_out = jnp.zeros((TOP_K, tm), jnp.int32)
    gate_out = jnp.zeros((TOP_K, tm), F32)
    gsum = jnp.zeros((GROUP_SIZE, tm), F32)
    for k in range(TOP_K):
        best = _bfly(functools.reduce(jnp.maximum, cand), jnp.maximum)
        pick = _bfly(functools.reduce(
            jnp.minimum, [jnp.where(cand[g] == best, eid[g], N_EXPERTS) for g in range(N_GROUPS)]),
            jnp.minimum)
        gv = _bfly(functools.reduce(
            jnp.add, [jnp.where(eid[g] == pick, sc_g[g], 0.0) for g in range(N_GROUPS)]), jnp.add)
        cand = [jnp.where(eid[g] == pick, -jnp.inf, cand[g]) for g in range(N_GROUPS)]
        idx_out = jnp.where(sub == k, pick, idx_out)
        gate_out = jnp.where(sub == k, gv, gate_out)
        gsum = gsum + gv
    idx_ref[...] = idx_out
    gate_ref[...] = gate_out / gsum * ROUTED_SCALE


ROW_TILE = 8


def _load_rows(ref, start, n):
    return jnp.concatenate(
        [ref[pl.ds(start + s, n, stride=ROW_TILE), :] for s in range(ROW_TILE)], axis=-1)


def _store_rows(ref, val):
    n, d = val.shape
    w = d // ROW_TILE
    for s in range(ROW_TILE):
        ref[pl.ds(s, n, stride=ROW_TILE), :] = val[:, s * w:(s + 1) * w]


def _tail(x, mix, g1, n2w, sc2, sh2, rwh_ref, rwl_ref, rb_ref, x1_ref, h2_ref, idx_ref, gate_ref):
    x1 = x + g1 * mix
    h2 = _modnorm(x1, n2w, sc2, sh2)
    x1_ref[...] = x1
    _store_rows(h2_ref, h2)
    _route(h2, rwh_ref, rwl_ref, rb_ref, idx_ref, gate_ref)


def _post_mix_kernel(x_ref, mla_ref, na_ref, wo_a_ref, wo_b_ref, g1_ref, n2_ref, sc2_ref, sh2_ref,
                     rwh_ref, rwl_ref, rb_ref, x1_ref, h2_ref, idx_ref, gate_ref):
    mix = _dot(mla_ref[...], wo_a_ref[...]) + _dot(na_ref[...], wo_b_ref[...])
    _tail(x_ref[...], mix, g1_ref[0], n2_ref[...], sc2_ref[0], sh2_ref[0],
          rwh_ref, rwl_ref, rb_ref, x1_ref, h2_ref, idx_ref, gate_ref)


def _tail_specs(t, d, tm, tiles_per_batch):
    tok = lambda w: pl.BlockSpec((tm, w), lambda i: (i, 0))
    mod = pl.BlockSpec((1, 1, d), lambda i: (i // tiles_per_batch, 0, 0))
    full2 = lambda shp: pl.BlockSpec(shp, lambda i: (0, 0))
    in_tail = [mod, full2((1, d)), mod, mod, full2((N_EXPERTS, d)), full2((N_EXPERTS, d)),
               full2((N_EXPERTS, 1))]
    out_specs = [tok(d), pl.BlockSpec((tm * ROW_TILE, d // ROW_TILE), lambda i: (i, 0)),
                 pl.BlockSpec((TOP_K, tm), lambda i: (0, i)),
                 pl.BlockSpec((TOP_K, tm), lambda i: (0, i))]
    out_shape = [jax.ShapeDtypeStruct((t, d), F32),
                 jax.ShapeDtypeStruct((t * ROW_TILE, d // ROW_TILE), F32),
                 jax.ShapeDtypeStruct((TOP_K, t), jnp.int32), jax.ShapeDtypeStruct((TOP_K, t), F32)]
    return tok, mod, full2, in_tail, out_specs, out_shape


def _post_mix(x2d, mla_o, na_o, wo_a, wo_b, g1, n2w, sc2, sh2, rwh, rwl, rb, seq, tm):
    t, d = x2d.shape
    tok, mod, full2, in_tail, out_specs, out_shape = _tail_specs(t, d, tm, seq // tm)
    return pl.pallas_call(
        _post_mix_kernel, grid=(t // tm,),
        in_specs=[tok(d), tok(mla_o.shape[1]), tok(na_o.shape[1]), full2(wo_a.shape), full2(wo_b.shape)]
        + in_tail,
        out_specs=out_specs, out_shape=out_shape, compiler_params=_cparams(1), name="post_mix",
    )(x2d, mla_o, na_o, wo_a, wo_b, g1, n2w, sc2, sh2, rwh, rwl, rb)


def _conv_pre_kernel(x_ref, sh_ref, sc_ref, n1_ref, w_ref, b_ref, g_ref):
    d = x_ref.shape[-1]
    h = _modnorm(x_ref[0], n1_ref[...], sc_ref[0], sh_ref[0])
    a = _dot(h.astype(BF16), w_ref[...]) + b_ref[...]
    g_ref[0] = a[:, :d] * _sigmoid(a[:, d:])


def _conv_pre(x, sh, sc, n1w, w, bias, tm):
    b, s, d = x.shape
    mod = pl.BlockSpec((1, 1, d), lambda bi, si: (bi, 0, 0))
    full = lambda a: pl.BlockSpec(a.shape, lambda bi, si: (0,) * a.ndim)
    tok = pl.BlockSpec((1, tm, d), lambda bi, si: (bi, si, 0))
    return pl.pallas_call(
        _conv_pre_kernel, grid=(b, s // tm),
        in_specs=[tok, mod, mod, full(n1w), full(w), full(bias)],
        out_specs=tok, out_shape=jax.ShapeDtypeStruct((b, s, d), F32),
        compiler_params=_cparams(2), name="conv_pre",
    )(x, sh, sc, n1w, w, bias)


def _conv_post_kernel(x_ref, g_ref, gp_ref, gn_ref, dw_ref, dwb_ref, lnw_ref, lnb_ref, w2_ref, b2_ref,
                      g1_ref, n2_ref, sc2_ref, sh2_ref, rwh_ref, rwl_ref, rb_ref,
                      x1_ref, h2_ref, idx_ref, gate_ref, gext_ref, gsh_ref, y_ref, *, tiles_per_batch):
    tm = x_ref.shape[0]
    si = pl.program_id(0) % tiles_per_batch
    zero = jnp.zeros_like(gp_ref[0])
    gext_ref[0:CONV_HALO] = jnp.where(si > 0, gp_ref[0], zero)
    gext_ref[CONV_HALO:CONV_HALO + tm] = g_ref[0]
    gext_ref[CONV_HALO + tm:] = jnp.where(si < tiles_per_batch - 1, gn_ref[0], zero)
    base = CONV_HALO - CONV_W // 2
    span = (base + CONV_W - 1) // SUBLANES * SUBLANES
    for r in range(1, SUBLANES):
        gsh_ref[r - 1] = gext_ref[r:r + span + tm, :]

    def chunk(c, carry):
        r0 = pl.multiple_of(c * CONV_CHUNK, CONV_CHUNK)
        acc = jnp.zeros((CONV_CHUNK, x_ref.shape[1]), F32) + dwb_ref[...]
        for r in range(SUBLANES):
            src = gext_ref if r == 0 else gsh_ref.at[r - 1]
            for a in range(0, span + 1, SUBLANES):
                j = a + r - base
                if 0 <= j < CONV_W:
                    acc = acc + dw_ref[j:j + 1, :] * src[pl.ds(r0 + a, CONV_CHUNK), :]
        y_ref[pl.ds(r0, CONV_CHUNK), :] = acc
        return carry

    lax.fori_loop(0, tm // CONV_CHUNK, chunk, 0)
    y = y_ref[...]
    mu = jnp.mean(y, axis=-1, keepdims=True)
    yc = y - mu
    var = jnp.mean(yc * yc, axis=-1, keepdims=True)
    z = _silu(yc * lax.rsqrt(var + EPS) * lnw_ref[...] + lnb_ref[...])
    mix = _dot(z.astype(BF16), w2_ref[...]) + b2_ref[...]
    _tail(x_ref[...], mix, g1_ref[0], n2_ref[...], sc2_ref[0], sh2_ref[0],
          rwh_ref, rwl_ref, rb_ref, x1_ref, h2_ref, idx_ref, gate_ref)


def _conv_post(x2d, g, dw, dwb, lnw, lnb, w2, b2, g1, n2w, sc2, sh2, rwh, rwl, rb, tm):
    t, d = x2d.shape
    b, s, _ = g.shape
    tpb = s // tm
    hb = tm // CONV_HALO
    nhb = s // CONV_HALO
    tok, mod, full2, in_tail, out_specs, out_shape = _tail_specs(t, d, tm, tpb)
    g_main = pl.BlockSpec((1, tm, d), lambda i: (i // tpb, i % tpb, 0))
    g_prev = pl.BlockSpec((1, CONV_HALO, d),
                          lambda i: (i // tpb, jnp.maximum((i % tpb) * hb - 1, 0), 0))
    g_next = pl.BlockSpec((1, CONV_HALO, d),
                          lambda i: (i // tpb, jnp.minimum((i % tpb + 1) * hb, nhb - 1), 0))
    return pl.pallas_call(
        functools.partial(_conv_post_kernel, tiles_per_batch=tpb), grid=(t // tm,),
        in_specs=[tok(d), g_main, g_prev, g_next, full2(dw.shape), full2(dwb.shape), full2(lnw.shape),
                  full2(lnb.shape), full2(w2.shape), full2(b2.shape)] + in_tail,
        out_specs=out_specs, out_shape=out_shape,
        scratch_shapes=[pltpu.VMEM((tm + 2 * CONV_HALO, d), F32),
                        pltpu.VMEM((SUBLANES - 1, tm + 2 * CONV_HALO - SUBLANES, d), F32),
                        pltpu.VMEM((tm, d), F32)],
        compiler_params=_cparams(1), name="conv_post",
    )(x2d, g, g, g, dw, dwb, lnw, lnb, w2, b2, g1, n2w, sc2, sh2, rwh, rwl, rb)


GATHER_UNROLL = 8


def _gather_tiles(idx_ref, src_ref, dst_ref, sem, n):
    def body(c, carry):
        for u in range(GATHER_UNROLL):
            j = c * GATHER_UNROLL + u
            s = idx_ref[0, 0, j]
            pltpu.make_async_copy(
                src_ref.at[pl.ds(pl.multiple_of(s * ROW_TILE, ROW_TILE), ROW_TILE)],
                dst_ref.at[pl.ds(pl.multiple_of(j * ROW_TILE, ROW_TILE), ROW_TILE)], sem).start()
        return carry
    lax.fori_loop(0, n // GATHER_UNROLL, body, 0)


def _gather_step(cur_idx_ref, next_idx_ref, src_ref, buf, sem, n):
    i = pl.program_id(0)
    slot = lax.rem(i, 2)

    @pl.when(i == 0)
    def _():
        _gather_tiles(cur_idx_ref, src_ref, buf.at[0], sem.at[0], n)

    @pl.when(i + 1 < pl.num_programs(0))
    def _():
        _gather_tiles(next_idx_ref, src_ref, buf.at[1 - slot], sem.at[1 - slot], n)

    pltpu.make_async_copy(src_ref.at[pl.ds(0, n * ROW_TILE)], buf.at[slot], sem.at[slot]).wait()
    return buf.at[slot]


def _gather_specs(steps, n):
    spec = lambda f: pl.BlockSpec((1, 1, n), f, memory_space=pltpu.SMEM)
    return [spec(lambda i, *_: (i, 0, 0)), spec(lambda i, *_: (jnp.minimum(i + 1, steps - 1), 0, 0))]


def _experts_kernel(blk_e_ref, tcur_ref, tnext_ref, h2_ref, wgu_ref, wd_ref, ys_ref, xbuf, sem):
    xcur = _gather_step(tcur_ref, tnext_ref, h2_ref, xbuf, sem, MOE_BLOCK)
    x = _load_rows(xcur, 0, MOE_BLOCK).astype(BF16)
    gu = _dot(x, wgu_ref[0])
    u = _silu(gu[:, :D_EXPERT]) * gu[:, D_EXPERT:]
    _store_rows(ys_ref, _dot(u.astype(BF16), wd_ref[0]))


def _experts(h2r, tok_slot, blk_e, wgu, wd):
    w = h2r.shape[1]
    d = w * ROW_TILE
    nb = blk_e.shape[0]
    rows = MOE_BLOCK * ROW_TILE
    tok3 = tok_slot.reshape(nb, 1, MOE_BLOCK)
    grid_spec = pltpu.PrefetchScalarGridSpec(
        num_scalar_prefetch=1, grid=(nb,),
        in_specs=_gather_specs(nb, MOE_BLOCK) + [
            pl.BlockSpec(memory_space=pl.ANY),
            pl.BlockSpec((1, d, 2 * D_EXPERT), lambda i, be: (be[i], 0, 0)),
            pl.BlockSpec((1, D_EXPERT, d), lambda i, be: (be[i], 0, 0))],
        out_specs=pl.BlockSpec((rows, w), lambda i, be: (i, 0)),
        scratch_shapes=[pltpu.VMEM((2, rows, w), F32), pltpu.SemaphoreType.DMA((2,))])
    return pl.pallas_call(
        _experts_kernel, grid_spec=grid_spec, out_shape=jax.ShapeDtypeStruct((nb * rows, w), F32),
        compiler_params=_cparams(1), name="moe_experts",
    )(blk_e, tok3, tok3, h2r, wgu, wd)


def _combine_kernel(dcur_ref, dnext_ref, x1_ref, h2_ref, gt_ref, g2_ref, wsgu_ref, wsd_ref, ys_ref,
                    o_ref, ybuf, sem):
    tm = x1_ref.shape[0]
    ycur = _gather_step(dcur_ref, dnext_ref, ys_ref, ybuf, sem, TOP_K * tm)
    gt = gt_ref[...]
    y = gt[:, 0:1] * _load_rows(ycur, 0, tm)
    for k in range(1, TOP_K):
        y = y + gt[:, k:k + 1] * _load_rows(ycur, k * tm * ROW_TILE, tm)
    hb = _load_rows(h2_ref, 0, tm).astype(BF16)
    gu = _dot(hb, wsgu_ref[...])
    ds = wsd_ref.shape[0]
    u = _silu(gu[:, :ds]) * gu[:, ds:]
    y = y + _dot(u.astype(BF16), wsd_ref[...])
    o_ref[...] = x1_ref[...] + g2_ref[0] * y


def _combine(x1, h2r, ys, dest_kt, gates_tk, g2, wsgu, wsd, seq, tm):
    t, d = x1.shape
    w = d // ROW_TILE
    tpb = seq // tm
    steps = t // tm
    tok = pl.BlockSpec((tm, d), lambda i: (i, 0))
    full2 = lambda a: pl.BlockSpec(a.shape, lambda i: (0, 0))
    return pl.pallas_call(
        _combine_kernel, grid=(steps,),
        in_specs=_gather_specs(steps, TOP_K * tm) + [
                  tok, pl.BlockSpec((tm * ROW_TILE, w), lambda i: (i, 0)),
                  pl.BlockSpec((tm, TOP_K), lambda i: (i, 0)),
                  pl.BlockSpec((1, 1, d), lambda i: (i // tpb, 0, 0)), full2(wsgu), full2(wsd),
                  pl.BlockSpec(memory_space=pl.ANY)],
        out_specs=tok, out_shape=jax.ShapeDtypeStruct((t, d), F32),
        scratch_shapes=[pltpu.VMEM((2, TOP_K * tm * ROW_TILE, w), F32), pltpu.SemaphoreType.DMA((2,))],
        compiler_params=_cparams(1), name="moe_combine",
    )(dest_kt, dest_kt, x1, h2r, gates_tk, g2, wsgu, wsd, ys)


def _moe(x1, h2r, idx, gates, g2, wgu, wd, wsgu, wsd, seq, tm):
    t, d = x1.shape
    a = TOP_K * t
    n_blocks = a // MOE_BLOCK + N_EXPERTS
    n_slots = n_blocks * MOE_BLOCK
    eq = idx[:, :, None] == jnp.arange(N_EXPERTS, dtype=jnp.int32)
    onehot = eq.sum(0).astype(jnp.int32)
    before = jnp.cumsum(onehot, axis=0) - onehot
    counts = onehot.sum(0)
    padded = (counts + MOE_BLOCK - 1) // MOE_BLOCK * MOE_BLOCK
    pend = jnp.cumsum(padded)
    pstart = pend - padded
    dest = jnp.where(eq, (before + pstart)[None], 0).sum(-1).astype(jnp.int32)
    blk_pos = jnp.arange(n_blocks, dtype=jnp.int32) * MOE_BLOCK
    blk_e = jnp.minimum((pend[None, :] <= blk_pos[:, None]).sum(-1), N_EXPERTS - 1).astype(jnp.int32)
    tok_ids = jnp.arange(t, dtype=jnp.int32)
    by_expert = lax.rem(jnp.sort((idx * t + tok_ids[None, :]).reshape(a)), t)
    start = jnp.cumsum(counts) - counts
    rank = blk_pos[:, None] + jnp.arange(MOE_BLOCK, dtype=jnp.int32)[None, :] - pstart[blk_e][:, None]
    valid = rank < counts[blk_e][:, None]
    tok_slot = jnp.where(valid, by_expert[jnp.clip(start[blk_e][:, None] + rank, 0, a - 1)], 0)

    ys = _experts(h2r, tok_slot.astype(jnp.int32), blk_e, wgu, wd)
    dest_kt = dest.reshape(TOP_K, t // tm, tm).transpose(1, 0, 2).reshape(t // tm, 1, TOP_K * tm)
    return _combine(x1, h2r, ys, dest_kt, gates.T, g2, wsgu, wsd, seq, tm)


def _axial_tables(seq, wq, wk):
    t = jnp.arange(seq)
    row = (t // GRID_W).astype(F32)
    col = (t % GRID_W).astype(F32)
    n_freq = MLA_ROPE // 4
    inv = ROPE_THETA ** (-jnp.arange(n_freq, dtype=F32) / n_freq)
    ang = jnp.concatenate([row[:, None] * inv, col[:, None] * inv], axis=-1)
    cos, sin = jnp.cos(ang), jnp.sin(ang)
    half = MLA_ROPE // 2
    pad = jnp.zeros((seq, HEAD_PAD - MLA_QK), F32)

    def tables(w, f):
        w0, w1, w2 = w[:MLA_NOPE], w[MLA_NOPE:MLA_NOPE + half], w[MLA_NOPE + half:]
        c_t = jnp.concatenate([jnp.broadcast_to(w0, (seq, MLA_NOPE)), w1 * cos, w2 * cos, pad], axis=-1)
        s_t = jnp.concatenate([jnp.zeros((seq, MLA_NOPE), F32), -w2 * sin, w1 * sin, pad], axis=-1)
        return c_t * f, s_t * f

    cq, sq = tables(wq, MLA_QK ** -0.5 * math.log2(math.e))
    ck, sk = tables(wk, 1.0)
    return cq, sq, ck, sk


def _attn_params(seq, w_in, qa_w, kva_w, w_q_b, w_kv_b, mq_w, mk_w, naq_w, nak_w):
    d = w_in.shape[0]
    half = MLA_ROPE // 2
    o_pe = MLA_Q_LORA + MLA_KV_LORA
    o_na = o_pe + MLA_ROPE
    z = lambda n: jnp.zeros((d, n), F32)
    pe1, pe2 = w_in[:, o_pe:o_pe + half], w_in[:, o_pe + half:o_na]
    pe = jnp.concatenate([z(MLA_NOPE), pe1, pe2, z(HEAD_PAD - MLA_QK)], axis=1)
    pes = jnp.concatenate([z(MLA_NOPE), pe2, pe1, z(HEAD_PAD - MLA_QK)], axis=1)
    w_in_r = jnp.concatenate([w_in[:, :o_pe], pe, pes, w_in[:, o_na:]], axis=1).astype(BF16)

    wq = w_q_b.reshape(MLA_Q_LORA, MLA_HEADS, MLA_QK)
    zq = lambda n: jnp.zeros((MLA_Q_LORA, MLA_HEADS, n), F32)
    wq_a = jnp.concatenate([wq, zq(HEAD_PAD - MLA_QK)], axis=-1)
    wq_b = jnp.concatenate([zq(MLA_NOPE), wq[..., MLA_NOPE + half:], wq[..., MLA_NOPE:MLA_NOPE + half],
                            zq(HEAD_PAD - MLA_QK)], axis=-1)
    wkv = w_kv_b.reshape(MLA_KV_LORA, MLA_HEADS, MLA_NOPE + MLA_V)
    zk = jnp.zeros((MLA_KV_LORA, MLA_HEADS, HEAD_PAD - MLA_NOPE), F32)
    wk = jnp.concatenate([wkv[..., :MLA_NOPE], zk], axis=-1)
    wv = jnp.concatenate([wkv[..., MLA_NOPE:], zk], axis=-1)
    hw = MLA_HEADS * HEAD_PAD
    cq, sq, ck, sk = _axial_tables(seq, mq_w, mk_w)
    return {
        "w_in": w_in_r, "qa_w": qa_w.reshape(1, -1), "kva_w": kva_w.reshape(1, -1),
        "wq_a": wq_a.reshape(MLA_Q_LORA, hw).astype(BF16), "wq_b": wq_b.reshape(MLA_Q_LORA, hw).astype(BF16),
        "wk": wk.reshape(MLA_KV_LORA, hw).astype(BF16), "wv": wv.reshape(MLA_KV_LORA, hw).astype(BF16),
        "cq": cq, "sq": sq, "ck": ck, "sk": sk,
        "na_qw": (jnp.tile(naq_w, NA_HEADS) * NA_HD ** -0.5).reshape(1, -1),
        "na_kw": jnp.tile(nak_w, NA_HEADS).reshape(1, -1),
    }


def _split_hi_lo(w):
    hi = w.astype(BF16)
    return hi, (w - hi.astype(F32)).astype(BF16)


TOKEN_TILE = 512
CONV_TILE = 256
COMBINE_TILE = 256
FLASH_TQ = 512
FLASH_TK = 2048


def kernel(x, c, norm1_w, norm2_w, ada_w, ada_b, w_in, q_a_norm_w, kv_a_norm_w, w_q_b, w_kv_b,
           mla_q_norm_w, mla_k_norm_w, na_q_norm_w, na_k_norm_w, na_rpb, w_out, pw1_w, pw1_b, dw_w, dw_b,
           conv_ln_w, conv_ln_b, pw2_w, pw2_b, router_w, router_b, exp_w_gate, exp_w_up, exp_w_down,
           sh_w_gate, sh_w_up, sh_w_down):
    b, s, d = x.shape
    depth = ada_w.shape[0]
    t = b * s
    mod = _ada_mod(c, ada_w, ada_b)
    row = lambda v: v.reshape(1, -1)
    for l in range(depth):
        m6 = mod[l].reshape(b, 1, 6, d)
        sh1, sc1, g1, sh2, sc2, g2 = (m6[:, :, i] for i in range(6))
        i = l // 2
        rwh, rwl = _split_hi_lo(router_w[l].T)
        rb = router_b[l].reshape(N_EXPERTS, 1)
        tail_args = (g1, row(norm2_w[l]), sc2, sh2, rwh, rwl, rb)
        if l % 2 == 0:
            prm = _attn_params(s, w_in[i], q_a_norm_w[i], kv_a_norm_w[i], w_q_b[i], w_kv_b[i],
                               mla_q_norm_w[i], mla_k_norm_w[i], na_q_norm_w[i], na_k_norm_w[i])
            qm, km, vm, qn, kn, vn = _attn_pre(x, sh1, sc1, row(norm1_w[l]), prm, TOKEN_TILE)
            mla_o = _mla_flash(qm, km, vm, FLASH_TQ, FLASH_TK)
            na_o = _na_attention(qn, kn, vn, _na_table(na_rpb[i], s // GRID_W))
            wo = w_out[i].astype(BF16)
            nm = MLA_HEADS * MLA_V
            x1, h2, idx, gates = _post_mix(x.reshape(t, d), mla_o.reshape(t, -1), na_o.reshape(t, -1),
                                           wo[:nm], wo[nm:], *tail_args, s, TOKEN_TILE)
        else:
            g = _conv_pre(x, sh1, sc1, row(norm1_w[l]), pw1_w[i].astype(BF16), row(pw1_b[i]), TOKEN_TILE)
            x1, h2, idx, gates = _conv_post(x.reshape(t, d), g, dw_w[i], row(dw_b[i]), row(conv_ln_w[i]),
                                            row(conv_ln_b[i]), pw2_w[i].astype(BF16), row(pw2_b[i]),
                                            *tail_args, CONV_TILE)
        wgu = jnp.concatenate([exp_w_gate[l], exp_w_up[l]], axis=-1).astype(BF16)
        wsgu = jnp.concatenate([sh_w_gate[l], sh_w_up[l]], axis=-1).astype(BF16)
        x = _moe(x1, h2, idx, gates, g2, wgu, exp_w_down[l].astype(BF16), wsgu,
                 sh_w_down[l].astype(BF16), s, COMBINE_TILE).reshape(b, s, d)
    return x
```

```python
import functools
import math

import jax
import jax.numpy as jnp
from jax import lax
from jax.experimental import pallas as pl
from jax.experimental.pallas import tpu as pltpu

F32 = jnp.float32
BF16 = jnp.bfloat16

EPS = 1e-6
GRID_W = 64
LANES = 128
HEAD_PAD = 128

MLA_HEADS = 8
MLA_Q_LORA = 256
MLA_KV_LORA = 128
MLA_NOPE = 64
MLA_ROPE = 32
MLA_V = 64
MLA_QK = MLA_NOPE + MLA_ROPE
ROPE_THETA = 10000.0

NA_HEADS = 8
NA_HD = 64
NA_KH = 8
NA_KW = 16
NA_QROWS = 8
NA_KROWS = 16

CONV_W = 31
CONV_HALO = 16
CONV_CHUNK = 32
SUBLANES = 8

N_EXPERTS = 64
TOP_K = 8
N_GROUPS = 8
TOPK_GROUPS = 4
GROUP_SIZE = N_EXPERTS // N_GROUPS
D_EXPERT = 256
ROUTED_SCALE = 2.5
MOE_BLOCK = 512

NEG_BIG = -1e30
VMEM_LIMIT = 56 * 1024 * 1024


def _cparams(n_axes):
    return pltpu.CompilerParams(dimension_semantics=("arbitrary",) * n_axes,
                                vmem_limit_bytes=VMEM_LIMIT)


def _dot(a, b):
    return jnp.dot(a, b, preferred_element_type=F32)


def _dot_nt(a, b):
    return lax.dot_general(a, b, (((1,), (1,)), ((), ())), preferred_element_type=F32)


def _rms(x):
    return x * lax.rsqrt(jnp.mean(x * x, axis=-1, keepdims=True) + EPS)


def _modnorm(x, w, sc, sh):
    return _rms(x) * w * (1.0 + sc) + sh


def _sigmoid(x):
    return 1.0 / (1.0 + jnp.exp(-x))


def _silu(x):
    return x * _sigmoid(x)


def _ada_kernel(c_ref, w_ref, b_ref, o_ref):
    c = c_ref[...]
    ca = _silu(c)
    o_ref[0] = jnp.dot(ca, w_ref[0], preferred_element_type=F32,
                       precision=lax.Precision.HIGHEST) + b_ref[0]


def _ada_mod(c, ada_w, ada_b):
    depth, d, d6 = ada_w.shape
    b = c.shape[0]
    cn = d
    return pl.pallas_call(
        _ada_kernel,
        grid=(depth, d6 // cn),
        in_specs=[pl.BlockSpec((b, d), lambda l, j: (0, 0)),
                  pl.BlockSpec((1, d, cn), lambda l, j: (l, 0, j)),
                  pl.BlockSpec((1, 1, cn), lambda l, j: (l, 0, j))],
        out_specs=pl.BlockSpec((1, b, cn), lambda l, j: (l, 0, j)),
        out_shape=jax.ShapeDtypeStruct((depth, b, d6), F32),
        compiler_params=_cparams(2),
        name="ada_mod",
    )(c, ada_w, ada_b.reshape(depth, 1, d6))


def _attn_pre_kernel(x_ref, sh_ref, sc_ref, n1_ref, win_ref, qaw_ref, kvaw_ref, wqa_ref, wqb_ref,
                     wk_ref, wv_ref, cq_ref, sq_ref, ck_ref, sk_ref, naq_ref, nak_ref,
                     qm_ref, km_ref, vm_ref, qn_ref, kn_ref, vn_ref):
    x = x_ref[0]
    h = _modnorm(x, n1_ref[...], sc_ref[0], sh_ref[0])
    p = _dot(h.astype(BF16), win_ref[...])
    o = 0
    cq = p[:, o:o + MLA_Q_LORA]; o += MLA_Q_LORA
    ckv = p[:, o:o + MLA_KV_LORA]; o += MLA_KV_LORA
    pe = p[:, o:o + HEAD_PAD]; o += HEAD_PAD
    pes = p[:, o:o + HEAD_PAD]; o += HEAD_PAD
    nw = NA_HEADS * NA_HD
    qn = p[:, o:o + nw]; o += nw
    kn = p[:, o:o + nw]; o += nw
    vn = p[:, o:o + nw]

    lane = lax.broadcasted_iota(jnp.int32, (1, HEAD_PAD), 1)

    cqn = (_rms(cq) * qaw_ref[...]).astype(BF16)
    qa = _dot(cqn, wqa_ref[...])
    qb = _dot(cqn, wqb_ref[...])
    cq_t, sq_t = cq_ref[...], sq_ref[...]
    for hh in range(MLA_HEADS):
        sl = slice(hh * HEAD_PAD, (hh + 1) * HEAD_PAD)
        a = qa[:, sl]
        r = lax.rsqrt(jnp.sum(a * a, axis=-1, keepdims=True) * (1.0 / MLA_QK) + EPS)
        qm_ref[0, :, sl] = (r * (a * cq_t + qb[:, sl] * sq_t)).astype(BF16)

    ckvn = (_rms(ckv) * kvaw_ref[...]).astype(BF16)
    kk = _dot(ckvn, wk_ref[...])
    vv = _dot(ckvn, wv_ref[...])
    ck_t, sk_t = ck_ref[...], sk_ref[...]
    pe_b = pes * sk_t
    one_col = jnp.where(lane == MLA_V, 1.0, 0.0)
    for hh in range(MLA_HEADS):
        sl = slice(hh * HEAD_PAD, (hh + 1) * HEAD_PAD)
        a = kk[:, sl] + pe
        r = lax.rsqrt(jnp.sum(a * a, axis=-1, keepdims=True) * (1.0 / MLA_QK) + EPS)
        km_ref[0, :, sl] = (r * (a * ck_t + pe_b)).astype(BF16)
        vm_ref[0, :, sl] = (vv[:, sl] + one_col).astype(BF16)

    lo = lane < NA_HD
    for src, w_ref, dst in ((qn, naq_ref, qn_ref), (kn, nak_ref, kn_ref)):
        for g in range(nw // LANES):
            sl = slice(g * LANES, (g + 1) * LANES)
            a = src[:, sl]
            sq = a * a
            tot = jnp.sum(sq, axis=-1, keepdims=True)
            s_lo = jnp.sum(jnp.where(lo, sq, 0.0), axis=-1, keepdims=True)
            r_lo = lax.rsqrt(s_lo * (1.0 / NA_HD) + EPS)
            r_hi = lax.rsqrt((tot - s_lo) * (1.0 / NA_HD) + EPS)
            dst[0, :, sl] = (a * jnp.where(lo, r_lo, r_hi) * w_ref[:, sl]).astype(BF16)
    vn_ref[0] = vn.astype(BF16)


def _attn_pre(x, sh, sc, n1w, prm, tm):
    b, s, d = x.shape
    hw = MLA_HEADS * HEAD_PAD
    nw = NA_HEADS * NA_HD
    full = lambda a: pl.BlockSpec(a.shape, lambda bi, si: (0,) * a.ndim)
    mod = pl.BlockSpec((1, 1, d), lambda bi, si: (bi, 0, 0))
    tab = pl.BlockSpec((tm, HEAD_PAD), lambda bi, si: (si, 0))
    tok = lambda w: pl.BlockSpec((1, tm, w), lambda bi, si: (bi, si, 0))
    ws = [prm[k] for k in ("w_in", "qa_w", "kva_w", "wq_a", "wq_b", "wk", "wv")]
    ins = [x, sh, sc, n1w] + ws + [prm["cq"], prm["sq"], prm["ck"], prm["sk"], prm["na_qw"], prm["na_kw"]]
    in_specs = ([tok(d), mod, mod, full(n1w)] + [full(a) for a in ws] + [tab] * 4
                + [full(prm["na_qw"]), full(prm["na_kw"])])
    out_shape = ([jax.ShapeDtypeStruct((b, s, hw), BF16)] * 3 + [jax.ShapeDtypeStruct((b, s, nw), BF16)] * 3)
    out_specs = [tok(hw)] * 3 + [tok(nw)] * 3
    return pl.pallas_call(
        _attn_pre_kernel, grid=(b, s // tm), in_specs=in_specs, out_specs=out_specs,
        out_shape=out_shape, compiler_params=_cparams(2), name="attn_pre",
    )(*ins)


def _mla_flash_kernel(q_ref, k_ref, v_ref, o_ref, *, tk):
    s_len = k_ref.shape[1]
    tq = q_ref.shape[1]
    sls = [slice(hh * HEAD_PAD, (hh + 1) * HEAD_PAD) for hh in range(2)]
    qs = [q_ref[0, :, sl] for sl in sls]

    def body(j, carry):
        off = pl.multiple_of(j * tk, tk)
        new = []
        for hh in range(2):
            m, acc = carry[hh]
            k = k_ref[0, pl.ds(off, tk), sls[hh]]
            v = v_ref[0, pl.ds(off, tk), sls[hh]]
            s = _dot_nt(qs[hh], k)
            m_new = jnp.maximum(m, jnp.max(s, axis=-1, keepdims=True))
            p = jnp.exp2(s - m_new)
            alpha = jnp.exp2(m - m_new)
            acc = alpha * acc + _dot(p.astype(BF16), v)
            new.append((m_new, acc))
        return tuple(new)

    m0 = jnp.full((tq, 1), -jnp.inf, F32)
    a0 = jnp.zeros((tq, HEAD_PAD), F32)
    res = lax.fori_loop(0, s_len // tk, body, ((m0, a0), (m0, a0)))
    outs = [acc / acc[:, MLA_V:MLA_V + 1] for _, acc in res]
    lane = lax.broadcasted_iota(jnp.int32, (1, HEAD_PAD), 1)
    shifted = pltpu.roll(outs[1], MLA_V, axis=1)
    o_ref[0] = jnp.where(lane < MLA_V, outs[0], shifted).astype(BF16)


def _mla_flash(q, k, v, tq, tk):
    b, s, hw = q.shape
    hp = MLA_HEADS // 2
    return pl.pallas_call(
        functools.partial(_mla_flash_kernel, tk=tk),
        grid=(b, hp, s // tq),
        in_specs=[pl.BlockSpec((1, tq, 2 * HEAD_PAD), lambda bi, h, qi: (bi, qi, h)),
                  pl.BlockSpec((1, s, 2 * HEAD_PAD), lambda bi, h, qi: (bi, 0, h)),
                  pl.BlockSpec((1, s, 2 * HEAD_PAD), lambda bi, h, qi: (bi, 0, h))],
        out_specs=pl.BlockSpec((1, tq, 2 * MLA_V), lambda bi, h, qi: (bi, qi, h)),
        out_shape=jax.ShapeDtypeStruct((b, s, MLA_HEADS * MLA_V), BF16),
        compiler_params=_cparams(3), name="mla_flash",
    )(q, k, v)


def _na_kernel(q_ref, k0, k1, k2, k3, v0, v1, v2, v3, t_ref, o_ref):
    q2 = q_ref[0]
    kc = jnp.concatenate([k0[0], k1[0], k2[0], k3[0]], axis=0)
    vc = jnp.concatenate([v0[0], v1[0], v2[0], v3[0]], axis=0)
    lane = lax.broadcasted_iota(jnp.int32, (1, LANES), 1)
    lo = lane < NA_HD
    outs = []
    for hh in range(2):
        qm = jnp.where(lo if hh == 0 else jnp.logical_not(lo), q2, jnp.zeros_like(q2))
        s = _dot_nt(qm, kc) + t_ref[0, hh]
        m = jnp.max(s, axis=-1, keepdims=True)
        p = jnp.exp(s - m)
        l = jnp.sum(p, axis=-1, keepdims=True)
        outs.append(_dot(p.astype(BF16), vc) / l)
    o_ref[0] = jnp.where(lo, outs[0], outs[1]).astype(BF16)


def _na_attention(q, k, v, table):
    b, s, nw = q.shape
    rows = s // GRID_W
    nblk = rows // NA_QROWS
    tq = NA_QROWS * GRID_W
    kp = 4
    tkp = NA_KROWS * GRID_W // kp
    kmax = s // tkp - kp
    hp = NA_HEADS // 2

    def kspec(j):
        return pl.BlockSpec((1, tkp, LANES),
                            lambda h, blk, bi: (bi, jnp.clip(2 * blk - 1, 0, kmax) + j, h))

    def tmap(h, blk, bi):
        cls = jnp.where(blk == 0, 0, jnp.where(blk == nblk - 1, 2, 1))
        return (cls, h, 0, 0)

    return pl.pallas_call(
        _na_kernel, grid=(hp, nblk, b),
        in_specs=[pl.BlockSpec((1, tq, LANES), lambda h, blk, bi: (bi, blk, h))]
        + [kspec(j) for j in range(kp)] * 2
        + [pl.BlockSpec((1, 2, tq, NA_KROWS * GRID_W), tmap)],
        out_specs=pl.BlockSpec((1, tq, LANES), lambda h, blk, bi: (bi, blk, h)),
        out_shape=jax.ShapeDtypeStruct((b, s, nw), BF16),
        compiler_params=_cparams(3), name="na_attn",
    )(q, k, k, k, k, v, v, v, v, table)


def _na_table(rpb, rows):
    nblk = rows // NA_QROWS
    tabs = []
    c = jnp.arange(GRID_W)
    cs = jnp.clip(c - NA_KW // 2, 0, GRID_W - NA_KW)
    vcol = (c[None, :] >= cs[:, None]) & (c[None, :] < cs[:, None] + NA_KW)
    co = jnp.clip(c[None, :] - c[:, None] + NA_KW - 1, 0, 2 * NA_KW - 2)
    for blk in (0, min(1, nblk - 1), nblk - 1):
        r = NA_QROWS * blk + jnp.arange(NA_QROWS)
        kb = min(max(NA_QROWS * blk - NA_KH // 2, 0), rows - NA_KROWS)
        krow = kb + jnp.arange(NA_KROWS)
        rs = jnp.clip(r - NA_KH // 2, 0, rows - NA_KH)
        vrow = (krow[None, :] >= rs[:, None]) & (krow[None, :] < rs[:, None] + NA_KH)
        ro = jnp.clip(krow[None, :] - r[:, None] + NA_KH - 1, 0, 2 * NA_KH - 2)
        bias = rpb[:, ro][:, :, :, co]
        bias = bias.transpose(0, 1, 3, 2, 4)
        valid = vrow[:, None, :, None] & vcol[None, :, None, :]
        tabs.append(jnp.where(valid[None], bias, NEG_BIG).reshape(
            rpb.shape[0], NA_QROWS * GRID_W, NA_KROWS * GRID_W))
    return jnp.stack(tabs).astype(F32)


def _bfly(x, op):
    for sh in (1, 2, 4):
        x = op(x, pltpu.roll(x, sh, axis=x.ndim - 2))
    return x


def _route(h2, rwh_ref, rwl_ref, rb_ref, idx_ref, gate_ref):
    tm = h2.shape[0]
    h_hi = h2.astype(BF16)
    h_lo = (h2 - h_hi.astype(F32)).astype(BF16)
    rwh = rwh_ref[...]
    logits = _dot_nt(rwh, h_hi) + _dot_nt(rwh, h_lo) + _dot_nt(rwl_ref[...], h_hi)
    scores = _sigmoid(logits)
    sel = scores + rb_ref[...]
    sub = lax.broadcasted_iota(jnp.int32, (GROUP_SIZE, tm), 0)
    sc_g = [scores[g * GROUP_SIZE:(g + 1) * GROUP_SIZE] for g in range(N_GROUPS)]
    sel_g = [sel[g * GROUP_SIZE:(g + 1) * GROUP_SIZE] for g in range(N_GROUPS)]

    gscore = []
    for x in sel_g:
        m1 = _bfly(x, jnp.maximum)
        first = _bfly(jnp.where(x == m1, sub, GROUP_SIZE), jnp.minimum)
        m2 = _bfly(jnp.where(sub == first, -jnp.inf, x), jnp.maximum)
        gscore.append(m1 + m2)

    gmask = [jnp.zeros((GROUP_SIZE, tm), jnp.bool_) for _ in range(N_GROUPS)]
    for _ in range(TOPK_GROUPS):
        best = functools.reduce(jnp.maximum, gscore)
        gidx = functools.reduce(jnp.minimum,
                                [jnp.where(gs == best, g, N_GROUPS) for g, gs in enumerate(gscore)])
        for g in range(N_GROUPS):
            hit = gidx == g
            gmask[g] = jnp.logical_or(gmask[g], hit)
            gscore[g] = jnp.where(hit, -jnp.inf, gscore[g])

    cand = [jnp.where(gmask[g], sel_g[g], -jnp.inf) for g in range(N_GROUPS)]
    eid = [sub + g * GROUP_SIZE for g in range(N_GROUPS)]
    idx_out = jnp.zeros((TOP_K, tm), jnp.int32)
    gate_out = jnp.zeros((TOP_K, tm), F32)
    gsum = jnp.zeros((GROUP_SIZE, tm), F32)
    for k in range(TOP_K):
        best = _bfly(functools.reduce(jnp.maximum, cand), jnp.maximum)
        pick = _bfly(functools.reduce(
            jnp.minimum, [jnp.where(cand[g] == best, eid[g], N_EXPERTS) for g in range(N_GROUPS)]),
            jnp.minimum)
        gv = _bfly(functools.reduce(
            jnp.add, [jnp.where(eid[g] == pick, sc_g[g], 0.0) for g in range(N_GROUPS)]), jnp.add)
        cand = [jnp.where(eid[g] == pick, -jnp.inf, cand[g]) for g in range(N_GROUPS)]
        idx_out = jnp.where(sub == k, pick, idx_out)
        gate_out = jnp.where(sub == k, gv, gate_out)
        gsum = gsum + gv
    idx_ref[...] = idx_out
    gate_ref[...] = gate_out / gsum * ROUTED_SCALE


ROW_TILE = 8


def _load_rows(ref, start, n):
    return jnp.concatenate(
        [ref[pl.ds(start + s, n, stride=ROW_TILE), :] for s in range(ROW_TILE)], axis=-1)


def _store_rows(ref, val):
    n, d = val.shape
    w = d // ROW_TILE
    for s in range(ROW_TILE):
        ref[pl.ds(s, n, stride=ROW_TILE), :] = val[:, s * w:(s + 1) * w]


def _tail(x, mix, g1, n2w, sc2, sh2, rwh_ref, rwl_ref, rb_ref, x1_ref, h2_ref, idx_ref, gate_ref):
    x1 = x + g1 * mix
    h2 = _modnorm(x1, n2w, sc2, sh2)
    x1_ref[...] = x1
    _store_rows(h2_ref, h2)
    _route(h2, rwh_ref, rwl_ref, rb_ref, idx_ref, gate_ref)


def _post_mix_kernel(x_ref, mla_ref, na_ref, wo_a_ref, wo_b_ref, g1_ref, n2_ref, sc2_ref, sh2_ref,
                     rwh_ref, rwl_ref, rb_ref, x1_ref, h2_ref, idx_ref, gate_ref):
    mix = _dot(mla_ref[...], wo_a_ref[...]) + _dot(na_ref[...], wo_b_ref[...])
    _tail(x_ref[...], mix, g1_ref[0], n2_ref[...], sc2_ref[0], sh2_ref[0],
          rwh_ref, rwl_ref, rb_ref, x1_ref, h2_ref, idx_ref, gate_ref)


def _tail_specs(t, d, tm, tiles_per_batch):
    tok = lambda w: pl.BlockSpec((tm, w), lambda i: (i, 0))
    mod = pl.BlockSpec((1, 1, d), lambda i: (i // tiles_per_batch, 0, 0))
    full2 = lambda shp: pl.BlockSpec(shp, lambda i: (0, 0))
    in_tail = [mod, full2((1, d)), mod, mod, full2((N_EXPERTS, d)), full2((N_EXPERTS, d)),
               full2((N_EXPERTS, 1))]
    out_specs = [tok(d), pl.BlockSpec((tm * ROW_TILE, d // ROW_TILE), lambda i: (i, 0)),
                 pl.BlockSpec((TOP_K, tm), lambda i: (0, i)),
                 pl.BlockSpec((TOP_K, tm), lambda i: (0, i))]
    out_shape = [jax.ShapeDtypeStruct((t, d), F32),
                 jax.ShapeDtypeStruct((t * ROW_TILE, d // ROW_TILE), F32),
                 jax.ShapeDtypeStruct((TOP_K, t), jnp.int32), jax.ShapeDtypeStruct((TOP_K, t), F32)]
    return tok, mod, full2, in_tail, out_specs, out_shape


def _post_mix(x2d, mla_o, na_o, wo_a, wo_b, g1, n2w, sc2, sh2, rwh, rwl, rb, seq, tm):
    t, d = x2d.shape
    tok, mod, full2, in_tail, out_specs, out_shape = _tail_specs(t, d, tm, seq // tm)
    return pl.pallas_call(
        _post_mix_kernel, grid=(t // tm,),
        in_specs=[tok(d), tok(mla_o.shape[1]), tok(na_o.shape[1]), full2(wo_a.shape), full2(wo_b.shape)]
        + in_tail,
        out_specs=out_specs, out_shape=out_shape, compiler_params=_cparams(1), name="post_mix",
    )(x2d, mla_o, na_o, wo_a, wo_b, g1, n2w, sc2, sh2, rwh, rwl, rb)


def _conv_pre_kernel(x_ref, sh_ref, sc_ref, n1_ref, w_ref, b_ref, g_ref):
    d = x_ref.shape[-1]
    h = _modnorm(x_ref[0], n1_ref[...], sc_ref[0], sh_ref[0])
    a = _dot(h.astype(BF16), w_ref[...]) + b_ref[...]
    g_ref[0] = a[:, :d] * _sigmoid(a[:, d:])


def _conv_pre(x, sh, sc, n1w, w, bias, tm):
    b, s, d = x.shape
    mod = pl.BlockSpec((1, 1, d), lambda bi, si: (bi, 0, 0))
    full = lambda a: pl.BlockSpec(a.shape, lambda bi, si: (0,) * a.ndim)
    tok = pl.BlockSpec((1, tm, d), lambda bi, si: (bi, si, 0))
    return pl.pallas_call(
        _conv_pre_kernel, grid=(b, s // tm),
        in_specs=[tok, mod, mod, full(n1w), full(w), full(bias)],
        out_specs=tok, out_shape=jax.ShapeDtypeStruct((b, s, d), F32),
        compiler_params=_cparams(2), name="conv_pre",
    )(x, sh, sc, n1w, w, bias)


def _conv_post_kernel(x_ref, g_ref, gp_ref, gn_ref, dw_ref, dwb_ref, lnw_ref, lnb_ref, w2_ref, b2_ref,
                      g1_ref, n2_ref, sc2_ref, sh2_ref, rwh_ref, rwl_ref, rb_ref,
                      x1_ref, h2_ref, idx_ref, gate_ref, gext_ref, gsh_ref, y_ref, *, tiles_per_batch):
    tm = x_ref.shape[0]
    si = pl.program_id(0) % tiles_per_batch
    zero = jnp.zeros_like(gp_ref[0])
    gext_ref[0:CONV_HALO] = jnp.where(si > 0, gp_ref[0], zero)
    gext_ref[CONV_HALO:CONV_HALO + tm] = g_ref[0]
    gext_ref[CONV_HALO + tm:] = jnp.where(si < tiles_per_batch - 1, gn_ref[0], zero)
    base = CONV_HALO - CONV_W // 2
    span = (base + CONV_W - 1) // SUBLANES * SUBLANES
    for r in range(1, SUBLANES):
        gsh_ref[r - 1] = gext_ref[r:r + span + tm, :]

    def chunk(c, carry):
        r0 = pl.multiple_of(c * CONV_CHUNK, CONV_CHUNK)
        acc = jnp.zeros((CONV_CHUNK, x_ref.shape[1]), F32) + dwb_ref[...]
        for r in range(SUBLANES):
            src = gext_ref if r == 0 else gsh_ref.at[r - 1]
            for a in range(0, span + 1, SUBLANES):
                j = a + r - base
                if 0 <= j < CONV_W:
                    acc = acc + dw_ref[j:j + 1, :] * src[pl.ds(r0 + a, CONV_CHUNK), :]
        y_ref[pl.ds(r0, CONV_CHUNK), :] = acc
        return carry

    lax.fori_loop(0, tm // CONV_CHUNK, chunk, 0)
    y = y_ref[...]
    mu = jnp.mean(y, axis=-1, keepdims=True)
    yc = y - mu
    var = jnp.mean(yc * yc, axis=-1, keepdims=True)
    z = _silu(yc * lax.rsqrt(var + EPS) * lnw_ref[...] + lnb_ref[...])
    mix = _dot(z.astype(BF16), w2_ref[...]) + b2_ref[...]
    _tail(x_ref[...], mix, g1_ref[0], n2_ref[...], sc2_ref[0], sh2_ref[0],
          rwh_ref, rwl_ref, rb_ref, x1_ref, h2_ref, idx_ref, gate_ref)


def _conv_post(x2d, g, dw, dwb, lnw, lnb, w2, b2, g1, n2w, sc2, sh2, rwh, rwl, rb, tm):
    t, d = x2d.shape
    b, s, _ = g.shape
    tpb = s // tm
    hb = tm // CONV_HALO
    nhb = s // CONV_HALO
    tok, mod, full2, in_tail, out_specs, out_shape = _tail_specs(t, d, tm, tpb)
    g_main = pl.BlockSpec((1, tm, d), lambda i: (i // tpb, i % tpb, 0))
    g_prev = pl.BlockSpec((1, CONV_HALO, d),
                          lambda i: (i // tpb, jnp.maximum((i % tpb) * hb - 1, 0), 0))
    g_next = pl.BlockSpec((1, CONV_HALO, d),
                          lambda i: (i // tpb, jnp.minimum((i % tpb + 1) * hb, nhb - 1), 0))
    return pl.pallas_call(
        functools.partial(_conv_post_kernel, tiles_per_batch=tpb), grid=(t // tm,),
        in_specs=[tok(d), g_main, g_prev, g_next, full2(dw.shape), full2(dwb.shape), full2(lnw.shape),
                  full2(lnb.shape), full2(w2.shape), full2(b2.shape)] + in_tail,
        out_specs=out_specs, out_shape=out_shape,
        scratch_shapes=[pltpu.VMEM((tm + 2 * CONV_HALO, d), F32),
                        pltpu.VMEM((SUBLANES - 1, tm + 2 * CONV_HALO - SUBLANES, d), F32),
                        pltpu.VMEM((tm, d), F32)],
        compiler_params=_cparams(1), name="conv_post",
    )(x2d, g, g, g, dw, dwb, lnw, lnb, w2, b2, g1, n2w, sc2, sh2, rwh, rwl, rb)


GATHER_UNROLL = 8


def _gather_tiles(idx_ref, src_ref, dst_ref, sem, n):
    def body(c, carry):
        for u in range(GATHER_UNROLL):
            j = c * GATHER_UNROLL + u
            s = idx_ref[0, 0, j]
            pltpu.make_async_copy(
                src_ref.at[pl.ds(pl.multiple_of(s * ROW_TILE, ROW_TILE), ROW_TILE)],
                dst_ref.at[pl.ds(pl.multiple_of(j * ROW_TILE, ROW_TILE), ROW_TILE)], sem).start()
        return carry
    lax.fori_loop(0, n // GATHER_UNROLL, body, 0)


def _gather_step(cur_idx_ref, next_idx_ref, src_ref, buf, sem, n):
    i = pl.program_id(0)
    slot = lax.rem(i, 2)

    @pl.when(i == 0)
    def _():
        _gather_tiles(cur_idx_ref, src_ref, buf.at[0], sem.at[0], n)

    @pl.when(i + 1 < pl.num_programs(0))
    def _():
        _gather_tiles(next_idx_ref, src_ref, buf.at[1 - slot], sem.at[1 - slot], n)

    pltpu.make_async_copy(src_ref.at[pl.ds(0, n * ROW_TILE)], buf.at[slot], sem.at[slot]).wait()
    return buf.at[slot]


def _gather_specs(steps, n):
    spec = lambda f: pl.BlockSpec((1, 1, n), f, memory_space=pltpu.SMEM)
    return [spec(lambda i, *_: (i, 0, 0)), spec(lambda i, *_: (jnp.minimum(i + 1, steps - 1), 0, 0))]


def _experts_kernel(blk_e_ref, tcur_ref, tnext_ref, h2_ref, wgu_ref, wd_ref, ys_ref, xbuf, sem):
    i = pl.program_id(0)
    last = pl.num_programs(0) - 1
    n_rows = MOE_BLOCK * ROW_TILE

    def wait(slot):
        pltpu.make_async_copy(h2_ref.at[pl.ds(0, n_rows)], xbuf.at[slot], sem.at[slot]).wait()

    @pl.when(i == 0)
    def _():
        _gather_tiles(tcur_ref, h2_ref, xbuf.at[0], sem.at[0], MOE_BLOCK)

    def step(slot):
        wait(slot)
        x = _load_rows(xbuf.at[slot], 0, MOE_BLOCK).astype(BF16)
        for j in range(MOE_BLOCK):
            s = tnext_ref[0, 0, j]
            pltpu.make_async_copy(
                h2_ref.at[pl.ds(pl.multiple_of(s * ROW_TILE, ROW_TILE), ROW_TILE)],
                xbuf.at[1 - slot, pl.ds(j * ROW_TILE, ROW_TILE)], sem.at[1 - slot]).start()
        gu = _dot(x, wgu_ref[0])
        u = _silu(gu[:, :D_EXPERT]) * gu[:, D_EXPERT:]
        _store_rows(ys_ref, _dot(u.astype(BF16), wd_ref[0]))

        @pl.when(i == last)
        def _():
            wait(1 - slot)

    for slot in range(2):
        pl.when(lax.rem(i, 2) == slot)(functools.partial(step, slot))


def _experts(h2r, tok_slot, blk_e, wgu, wd):
    w = h2r.shape[1]
    d = w * ROW_TILE
    nb = blk_e.shape[0]
    rows = MOE_BLOCK * ROW_TILE
    tok3 = tok_slot.reshape(nb, 1, MOE_BLOCK)
    grid_spec = pltpu.PrefetchScalarGridSpec(
        num_scalar_prefetch=1, grid=(nb,),
        in_specs=_gather_specs(nb, MOE_BLOCK) + [
            pl.BlockSpec(memory_space=pl.ANY),
            pl.BlockSpec((1, d, 2 * D_EXPERT), lambda i, be: (be[i], 0, 0)),
            pl.BlockSpec((1, D_EXPERT, d), lambda i, be: (be[i], 0, 0))],
        out_specs=pl.BlockSpec((rows, w), lambda i, be: (i, 0)),
        scratch_shapes=[pltpu.VMEM((2, rows, w), F32), pltpu.SemaphoreType.DMA((2,))])
    return pl.pallas_call(
        _experts_kernel, grid_spec=grid_spec, out_shape=jax.ShapeDtypeStruct((nb * rows, w), F32),
        compiler_params=_cparams(1), name="moe_experts",
    )(blk_e, tok3, tok3, h2r, wgu, wd)


def _combine_kernel(dcur_ref, dnext_ref, x1_ref, h2_ref, gt_ref, g2_ref, wsgu_ref, wsd_ref, ys_ref,
                    o_ref, ybuf, sem):
    tm = x1_ref.shape[0]
    ycur = _gather_step(dcur_ref, dnext_ref, ys_ref, ybuf, sem, TOP_K * tm)
    gt = gt_ref[...]
    y = gt[:, 0:1] * _load_rows(ycur, 0, tm)
    for k in range(1, TOP_K):
        y = y + gt[:, k:k + 1] * _load_rows(ycur, k * tm * ROW_TILE, tm)
    hb = _load_rows(h2_ref, 0, tm).astype(BF16)
    gu = _dot(hb, wsgu_ref[...])
    ds = wsd_ref.shape[0]
    u = _silu(gu[:, :ds]) * gu[:, ds:]
    y = y + _dot(u.astype(BF16), wsd_ref[...])
    o_ref[...] = x1_ref[...] + g2_ref[0] * y


def _combine(x1, h2r, ys, dest_kt, gates_tk, g2, wsgu, wsd, seq, tm):
    t, d = x1.shape
    w = d // ROW_TILE
    tpb = seq // tm
    steps = t // tm
    tok = pl.BlockSpec((tm, d), lambda i: (i, 0))
    full2 = lambda a: pl.BlockSpec(a.shape, lambda i: (0, 0))
    return pl.pallas_call(
        _combine_kernel, grid=(steps,),
        in_specs=_gather_specs(steps, TOP_K * tm) + [
                  tok, pl.BlockSpec((tm * ROW_TILE, w), lambda i: (i, 0)),
                  pl.BlockSpec((tm, TOP_K), lambda i: (i, 0)),
                  pl.BlockSpec((1, 1, d), lambda i: (i // tpb, 0, 0)), full2(wsgu), full2(wsd),
                  pl.BlockSpec(memory_space=pl.ANY)],
        out_specs=tok, out_shape=jax.ShapeDtypeStruct((t, d), F32),
        scratch_shapes=[pltpu.VMEM((2, TOP_K * tm * ROW_TILE, w), F32), pltpu.SemaphoreType.DMA((2,))],
        compiler_params=_cparams(1), name="moe_combine",
    )(dest_kt, dest_kt, x1, h2r, gates_tk, g2, wsgu, wsd, ys)


def _moe(x1, h2r, idx, gates, g2, wgu, wd, wsgu, wsd, seq, tm):
    t, d = x1.shape
    a = TOP_K * t
    n_blocks = a // MOE_BLOCK + N_EXPERTS
    n_slots = n_blocks * MOE_BLOCK
    eq = idx[:, :, None] == jnp.arange(N_EXPERTS, dtype=jnp.int32)
    onehot = eq.sum(0).astype(jnp.int32)
    before = jnp.cumsum(onehot, axis=0) - onehot
    counts = onehot.sum(0)
    padded = (counts + MOE_BLOCK - 1) // MOE_BLOCK * MOE_BLOCK
    pend = jnp.cumsum(padded)
    pstart = pend - padded
    dest = jnp.where(eq, (before + pstart)[None], 0).sum(-1).astype(jnp.int32)
    blk_pos = jnp.arange(n_blocks, dtype=jnp.int32) * MOE_BLOCK
    blk_e = jnp.minimum((pend[None, :] <= blk_pos[:, None]).sum(-1), N_EXPERTS - 1).astype(jnp.int32)
    tok_ids = jnp.arange(t, dtype=jnp.int32)
    by_expert = lax.rem(jnp.sort((idx * t + tok_ids[None, :]).reshape(a)), t)
    start = jnp.cumsum(counts) - counts
    rank = blk_pos[:, None] + jnp.arange(MOE_BLOCK, dtype=jnp.int32)[None, :] - pstart[blk_e][:, None]
    valid = rank < counts[blk_e][:, None]
    tok_slot = jnp.where(valid, by_expert[jnp.clip(start[blk_e][:, None] + rank, 0, a - 1)], 0)

    ys = _experts(h2r, tok_slot.astype(jnp.int32), blk_e, wgu, wd)
    dest_kt = dest.reshape(TOP_K, t // tm, tm).transpose(1, 0, 2).reshape(t // tm, 1, TOP_K * tm)
    return _combine(x1, h2r, ys, dest_kt, gates.T, g2, wsgu, wsd, seq, tm)


def _axial_tables(seq, wq, wk):
    t = jnp.arange(seq)
    row = (t // GRID_W).astype(F32)
    col = (t % GRID_W).astype(F32)
    n_freq = MLA_ROPE // 4
    inv = ROPE_THETA ** (-jnp.arange(n_freq, dtype=F32) / n_freq)
    ang = jnp.concatenate([row[:, None] * inv, col[:, None] * inv], axis=-1)
    cos, sin = jnp.cos(ang), jnp.sin(ang)
    half = MLA_ROPE // 2
    pad = jnp.zeros((seq, HEAD_PAD - MLA_QK), F32)

    def tables(w, f):
        w0, w1, w2 = w[:MLA_NOPE], w[MLA_NOPE:MLA_NOPE + half], w[MLA_NOPE + half:]
        c_t = jnp.concatenate([jnp.broadcast_to(w0, (seq, MLA_NOPE)), w1 * cos, w2 * cos, pad], axis=-1)
        s_t = jnp.concatenate([jnp.zeros((seq, MLA_NOPE), F32), -w2 * sin, w1 * sin, pad], axis=-1)
        return c_t * f, s_t * f

    cq, sq = tables(wq, MLA_QK ** -0.5 * math.log2(math.e))
    ck, sk = tables(wk, 1.0)
    return cq, sq, ck, sk


def _attn_params(seq, w_in, qa_w, kva_w, w_q_b, w_kv_b, mq_w, mk_w, naq_w, nak_w):
    d = w_in.shape[0]
    half = MLA_ROPE // 2
    o_pe = MLA_Q_LORA + MLA_KV_LORA
    o_na = o_pe + MLA_ROPE
    z = lambda n: jnp.zeros((d, n), F32)
    pe1, pe2 = w_in[:, o_pe:o_pe + half], w_in[:, o_pe + half:o_na]
    pe = jnp.concatenate([z(MLA_NOPE), pe1, pe2, z(HEAD_PAD - MLA_QK)], axis=1)
    pes = jnp.concatenate([z(MLA_NOPE), pe2, pe1, z(HEAD_PAD - MLA_QK)], axis=1)
    w_in_r = jnp.concatenate([w_in[:, :o_pe], pe, pes, w_in[:, o_na:]], axis=1).astype(BF16)

    wq = w_q_b.reshape(MLA_Q_LORA, MLA_HEADS, MLA_QK)
    zq = lambda n: jnp.zeros((MLA_Q_LORA, MLA_HEADS, n), F32)
    wq_a = jnp.concatenate([wq, zq(HEAD_PAD - MLA_QK)], axis=-1)
    wq_b = jnp.concatenate([zq(MLA_NOPE), wq[..., MLA_NOPE + half:], wq[..., MLA_NOPE:MLA_NOPE + half],
                            zq(HEAD_PAD - MLA_QK)], axis=-1)
    wkv = w_kv_b.reshape(MLA_KV_LORA, MLA_HEADS, MLA_NOPE + MLA_V)
    zk = jnp.zeros((MLA_KV_LORA, MLA_HEADS, HEAD_PAD - MLA_NOPE), F32)
    wk = jnp.concatenate([wkv[..., :MLA_NOPE], zk], axis=-1)
    wv = jnp.concatenate([wkv[..., MLA_NOPE:], zk], axis=-1)
    hw = MLA_HEADS * HEAD_PAD
    cq, sq, ck, sk = _axial_tables(seq, mq_w, mk_w)
    return {
        "w_in": w_in_r, "qa_w": qa_w.reshape(1, -1), "kva_w": kva_w.reshape(1, -1),
        "wq_a": wq_a.reshape(MLA_Q_LORA, hw).astype(BF16), "wq_b": wq_b.reshape(MLA_Q_LORA, hw).astype(BF16),
        "wk": wk.reshape(MLA_KV_LORA, hw).astype(BF16), "wv": wv.reshape(MLA_KV_LORA, hw).astype(BF16),
        "cq": cq, "sq": sq, "ck": ck, "sk": sk,
        "na_qw": (jnp.tile(naq_w, NA_HEADS) * NA_HD ** -0.5).reshape(1, -1),
        "na_kw": jnp.tile(nak_w, NA_HEADS).reshape(1, -1),
    }


def _split_hi_lo(w):
    hi = w.astype(BF16)
    return hi, (w - hi.astype(F32)).astype(BF16)


TOKEN_TILE = 512
CONV_TILE = 256
COMBINE_TILE = 256
FLASH_TQ = 512
FLASH_TK = 2048


def kernel(x, c, norm1_w, norm2_w, ada_w, ada_b, w_in, q_a_norm_w, kv_a_norm_w, w_q_b, w_kv_b,
           mla_q_norm_w, mla_k_norm_w, na_q_norm_w, na_k_norm_w, na_rpb, w_out, pw1_w, pw1_b, dw_w, dw_b,
           conv_ln_w, conv_ln_b, pw2_w, pw2_b, router_w, router_b, exp_w_gate, exp_w_up, exp_w_down,
           sh_w_gate, sh_w_up, sh_w_down):
    b, s, d = x.shape
    depth = ada_w.shape[0]
    t = b * s
    mod = _ada_mod(c, ada_w, ada_b)
    row = lambda v: v.reshape(1, -1)
    for l in range(depth):
        m6 = mod[l].reshape(b, 1, 6, d)
        sh1, sc1, g1, sh2, sc2, g2 = (m6[:, :, i] for i in range(6))
        i = l // 2
        rwh, rwl = _split_hi_lo(router_w[l].T)
        rb = router_b[l].reshape(N_EXPERTS, 1)
        tail_args = (g1, row(norm2_w[l]), sc2, sh2, rwh, rwl, rb)
        if l % 2 == 0:
            prm = _attn_params(s, w_in[i], q_a_norm_w[i], kv_a_norm_w[i], w_q_b[i], w_kv_b[i],
                               mla_q_norm_w[i], mla_k_norm_w[i], na_q_norm_w[i], na_k_norm_w[i])
            qm, km, vm, qn, kn, vn = _attn_pre(x, sh1, sc1, row(norm1_w[l]), prm, TOKEN_TILE)
            mla_o = _mla_flash(qm, km, vm, FLASH_TQ, FLASH_TK)
            na_o = _na_attention(qn, kn, vn, _na_table(na_rpb[i], s // GRID_W))
            wo = w_out[i].astype(BF16)
            nm = MLA_HEADS * MLA_V
            x1, h2, idx, gates = _post_mix(x.reshape(t, d), mla_o.reshape(t, -1), na_o.reshape(t, -1),
                                           wo[:nm], wo[nm:], *tail_args, s, TOKEN_TILE)
        else:
            g = _conv_pre(x, sh1, sc1, row(norm1_w[l]), pw1_w[i].astype(BF16), row(pw1_b[i]), TOKEN_TILE)
            x1, h2, idx, gates = _conv_post(x.reshape(t, d), g, dw_w[i], row(dw_b[i]), row(conv_ln_w[i]),
                                            row(conv_ln_b[i]), pw2_w[i].astype(BF16), row(pw2_b[i]),
                                            *tail_args, CONV_TILE)
        wgu = jnp.concatenate([exp_w_gate[l], exp_w_up[l]], axis=-1).astype(BF16)
        wsgu = jnp.concatenate([sh_w_gate[l], sh_w_up[l]], axis=-1).astype(BF16)
        x = _moe(x1, h2, idx, gates, g2, wgu, exp_w_down[l].astype(BF16), wsgu,
                 sh_w_down[l].astype(BF16), s, COMBINE_TILE).reshape(b, s, d)
    return x
```

```python
import functools
import math

import jax
import jax.numpy as jnp
from jax import lax
from jax.experimental import pallas as pl
from jax.experimental.pallas import tpu as pltpu

F32 = jnp.float32
BF16 = jnp.bfloat16

EPS = 1e-6
GRID_W = 64
LANES = 128
HEAD_PAD = 128

MLA_HEADS = 8
MLA_Q_LORA = 256
MLA_KV_LORA = 128
MLA_NOPE = 64
MLA_ROPE = 32
MLA_V = 64
MLA_QK = MLA_NOPE + MLA_ROPE
ROPE_THETA = 10000.0

NA_HEADS = 8
NA_HD = 64
NA_KH = 8
NA_KW = 16
NA_QROWS = 8
NA_KROWS = 16

CONV_W = 31
CONV_HALO = 16
CONV_CHUNK = 32
SUBLANES = 8

N_EXPERTS = 64
TOP_K = 8
N_GROUPS = 8
TOPK_GROUPS = 4
GROUP_SIZE = N_EXPERTS // N_GROUPS
D_EXPERT = 256
ROUTED_SCALE = 2.5
MOE_BLOCK = 512

NEG_BIG = -1e30
VMEM_LIMIT = 56 * 1024 * 1024


def _cparams(n_axes):
    return pltpu.CompilerParams(dimension_semantics=("arbitrary",) * n_axes,
                                vmem_limit_bytes=VMEM_LIMIT)


def _dot(a, b):
    return jnp.dot(a, b, preferred_element_type=F32)


def _dot_nt(a, b):
    return lax.dot_general(a, b, (((1,), (1,)), ((), ())), preferred_element_type=F32)


def _rms(x):
    return x * lax.rsqrt(jnp.mean(x * x, axis=-1, keepdims=True) + EPS)


def _modnorm(x, w, sc, sh):
    return _rms(x) * w * (1.0 + sc) + sh


def _sigmoid(x):
    return 1.0 / (1.0 + jnp.exp(-x))


def _silu(x):
    return x * _sigmoid(x)


def _ada_kernel(c_ref, w_ref, b_ref, o_ref):
    c = c_ref[...]
    ca = _silu(c)
    o_ref[0] = jnp.dot(ca, w_ref[0], preferred_element_type=F32,
                       precision=lax.Precision.HIGHEST) + b_ref[0]


def _ada_mod(c, ada_w, ada_b):
    depth, d, d6 = ada_w.shape
    b = c.shape[0]
    cn = d
    return pl.pallas_call(
        _ada_kernel,
        grid=(depth, d6 // cn),
        in_specs=[pl.BlockSpec((b, d), lambda l, j: (0, 0)),
                  pl.BlockSpec((1, d, cn), lambda l, j: (l, 0, j)),
                  pl.BlockSpec((1, 1, cn), lambda l, j: (l, 0, j))],
        out_specs=pl.BlockSpec((1, b, cn), lambda l, j: (l, 0, j)),
        out_shape=jax.ShapeDtypeStruct((depth, b, d6), F32),
        compiler_params=_cparams(2),
        name="ada_mod",
    )(c, ada_w, ada_b.reshape(depth, 1, d6))


def _attn_pre_kernel(x_ref, sh_ref, sc_ref, n1_ref, win_ref, qaw_ref, kvaw_ref, wqa_ref, wqb_ref,
                     wk_ref, wv_ref, cq_ref, sq_ref, ck_ref, sk_ref, naq_ref, nak_ref,
                     qm_ref, km_ref, vm_ref, qn_ref, kn_ref, vn_ref):
    x = x_ref[0]
    h = _modnorm(x, n1_ref[...], sc_ref[0], sh_ref[0])
    p = _dot(h.astype(BF16), win_ref[...])
    o = 0
    cq = p[:, o:o + MLA_Q_LORA]; o += MLA_Q_LORA
    ckv = p[:, o:o + MLA_KV_LORA]; o += MLA_KV_LORA
    pe = p[:, o:o + HEAD_PAD]; o += HEAD_PAD
    pes = p[:, o:o + HEAD_PAD]; o += HEAD_PAD
    nw = NA_HEADS * NA_HD
    qn = p[:, o:o + nw]; o += nw
    kn = p[:, o:o + nw]; o += nw
    vn = p[:, o:o + nw]

    lane = lax.broadcasted_iota(jnp.int32, (1, HEAD_PAD), 1)

    cqn = (_rms(cq) * qaw_ref[...]).astype(BF16)
    qa = _dot(cqn, wqa_ref[...])
    qb = _dot(cqn, wqb_ref[...])
    cq_t, sq_t = cq_ref[...], sq_ref[...]
    for hh in range(MLA_HEADS):
        sl = slice(hh * HEAD_PAD, (hh + 1) * HEAD_PAD)
        a = qa[:, sl]
        r = lax.rsqrt(jnp.sum(a * a, axis=-1, keepdims=True) * (1.0 / MLA_QK) + EPS)
        qm_ref[0, :, sl] = (r * (a * cq_t + qb[:, sl] * sq_t)).astype(BF16)

    ckvn = (_rms(ckv) * kvaw_ref[...]).astype(BF16)
    kk = _dot(ckvn, wk_ref[...])
    vv = _dot(ckvn, wv_ref[...])
    ck_t, sk_t = ck_ref[...], sk_ref[...]
    pe_b = pes * sk_t
    one_col = jnp.where(lane == MLA_V, 1.0, 0.0)
    for hh in range(MLA_HEADS):
        sl = slice(hh * HEAD_PAD, (hh + 1) * HEAD_PAD)
        a = kk[:, sl] + pe
        r = lax.rsqrt(jnp.sum(a * a, axis=-1, keepdims=True) * (1.0 / MLA_QK) + EPS)
        km_ref[0, :, sl] = (r * (a * ck_t + pe_b)).astype(BF16)
        vm_ref[0, :, sl] = (vv[:, sl] + one_col).astype(BF16)

    lo = lane < NA_HD
    for src, w_ref, dst in ((qn, naq_ref, qn_ref), (kn, nak_ref, kn_ref)):
        for g in range(nw // LANES):
            sl = slice(g * LANES, (g + 1) * LANES)
            a = src[:, sl]
            sq = a * a
            tot = jnp.sum(sq, axis=-1, keepdims=True)
            s_lo = jnp.sum(jnp.where(lo, sq, 0.0), axis=-1, keepdims=True)
            r_lo = lax.rsqrt(s_lo * (1.0 / NA_HD) + EPS)
            r_hi = lax.rsqrt((tot - s_lo) * (1.0 / NA_HD) + EPS)
            dst[0, :, sl] = (a * jnp.where(lo, r_lo, r_hi) * w_ref[:, sl]).astype(BF16)
    vn_ref[0] = vn.astype(BF16)


def _attn_pre(x, sh, sc, n1w, prm, tm):
    b, s, d = x.shape
    hw = MLA_HEADS * HEAD_PAD
    nw = NA_HEADS * NA_HD
    full = lambda a: pl.BlockSpec(a.shape, lambda bi, si: (0,) * a.ndim)
    mod = pl.BlockSpec((1, 1, d), lambda bi, si: (bi, 0, 0))
    tab = pl.BlockSpec((tm, HEAD_PAD), lambda bi, si: (si, 0))
    tok = lambda w: pl.BlockSpec((1, tm, w), lambda bi, si: (bi, si, 0))
    ws = [prm[k] for k in ("w_in", "qa_w", "kva_w", "wq_a", "wq_b", "wk", "wv")]
    ins = [x, sh, sc, n1w] + ws + [prm["cq"], prm["sq"], prm["ck"], prm["sk"], prm["na_qw"], prm["na_kw"]]
    in_specs = ([tok(d), mod, mod, full(n1w)] + [full(a) for a in ws] + [tab] * 4
                + [full(prm["na_qw"]), full(prm["na_kw"])])
    out_shape = ([jax.ShapeDtypeStruct((b, s, hw), BF16)] * 3 + [jax.ShapeDtypeStruct((b, s, nw), BF16)] * 3)
    out_specs = [tok(hw)] * 3 + [tok(nw)] * 3
    return pl.pallas_call(
        _attn_pre_kernel, grid=(b, s // tm), in_specs=in_specs, out_specs=out_specs,
        out_shape=out_shape, compiler_params=_cparams(2), name="attn_pre",
    )(*ins)


def _mla_flash_kernel(q_ref, k_ref, v_ref, o_ref, *, tk):
    s_len = k_ref.shape[1]
    tq = q_ref.shape[1]
    sls = [slice(hh * HEAD_PAD, (hh + 1) * HEAD_PAD) for hh in range(2)]
    qs = [q_ref[0, :, sl] for sl in sls]

    def body(j, carry):
        off = pl.multiple_of(j * tk, tk)
        new = []
        for hh in range(2):
            m, acc = carry[hh]
            k = k_ref[0, pl.ds(off, tk), sls[hh]]
            v = v_ref[0, pl.ds(off, tk), sls[hh]]
            s = _dot_nt(qs[hh], k)
            m_new = jnp.maximum(m, jnp.max(s, axis=-1, keepdims=True))
            p = jnp.exp2(s - m_new)
            alpha = jnp.exp2(m - m_new)
            acc = alpha * acc + _dot(p.astype(BF16), v)
            new.append((m_new, acc))
        return tuple(new)

    m0 = jnp.full((tq, 1), -jnp.inf, F32)
    a0 = jnp.zeros((tq, HEAD_PAD), F32)
    res = lax.fori_loop(0, s_len // tk, body, ((m0, a0), (m0, a0)))
    outs = [acc / acc[:, MLA_V:MLA_V + 1] for _, acc in res]
    lane = lax.broadcasted_iota(jnp.int32, (1, HEAD_PAD), 1)
    shifted = pltpu.roll(outs[1], MLA_V, axis=1)
    o_ref[0] = jnp.where(lane < MLA_V, outs[0], shifted).astype(BF16)


def _mla_flash(q, k, v, tq, tk):
    b, s, hw = q.shape
    hp = MLA_HEADS // 2
    return pl.pallas_call(
        functools.partial(_mla_flash_kernel, tk=tk),
        grid=(b, hp, s // tq),
        in_specs=[pl.BlockSpec((1, tq, 2 * HEAD_PAD), lambda bi, h, qi: (bi, qi, h)),
                  pl.BlockSpec((1, s, 2 * HEAD_PAD), lambda bi, h, qi: (bi, 0, h)),
                  pl.BlockSpec((1, s, 2 * HEAD_PAD), lambda bi, h, qi: (bi, 0, h))],
        out_specs=pl.BlockSpec((1, tq, 2 * MLA_V), lambda bi, h, qi: (bi, qi, h)),
        out_shape=jax.ShapeDtypeStruct((b, s, MLA_HEADS * MLA_V), BF16),
        compiler_params=_cparams(3), name="mla_flash",
    )(q, k, v)


def _na_kernel(q_ref, k0, k1, k2, k3, v0, v1, v2, v3, t_ref, o_ref):
    q2 = q_ref[0]
    kc = jnp.concatenate([k0[0], k1[0], k2[0], k3[0]], axis=0)
    vc = jnp.concatenate([v0[0], v1[0], v2[0], v3[0]], axis=0)
    lane = lax.broadcasted_iota(jnp.int32, (1, LANES), 1)
    lo = lane < NA_HD
    outs = []
    for hh in range(2):
        qm = jnp.where(lo if hh == 0 else jnp.logical_not(lo), q2, jnp.zeros_like(q2))
        s = _dot_nt(qm, kc) + t_ref[0, hh]
        m = jnp.max(s, axis=-1, keepdims=True)
        p = jnp.exp(s - m)
        l = jnp.sum(p, axis=-1, keepdims=True)
        outs.append(_dot(p.astype(BF16), vc) / l)
    o_ref[0] = jnp.where(lo, outs[0], outs[1]).astype(BF16)


def _na_attention(q, k, v, table):
    b, s, nw = q.shape
    rows = s // GRID_W
    nblk = rows // NA_QROWS
    tq = NA_QROWS * GRID_W
    kp = 4
    tkp = NA_KROWS * GRID_W // kp
    kmax = s // tkp - kp
    hp = NA_HEADS // 2

    def kspec(j):
        return pl.BlockSpec((1, tkp, LANES),
                            lambda h, blk, bi: (bi, jnp.clip(2 * blk - 1, 0, kmax) + j, h))

    def tmap(h, blk, bi):
        cls = jnp.where(blk == 0, 0, jnp.where(blk == nblk - 1, 2, 1))
        return (cls, h, 0, 0)

    return pl.pallas_call(
        _na_kernel, grid=(hp, nblk, b),
        in_specs=[pl.BlockSpec((1, tq, LANES), lambda h, blk, bi: (bi, blk, h))]
        + [kspec(j) for j in range(kp)] * 2
        + [pl.BlockSpec((1, 2, tq, NA_KROWS * GRID_W), tmap)],
        out_specs=pl.BlockSpec((1, tq, LANES), lambda h, blk, bi: (bi, blk, h)),
        out_shape=jax.ShapeDtypeStruct((b, s, nw), BF16),
        compiler_params=_cparams(3), name="na_attn",
    )(q, k, k, k, k, v, v, v, v, table)


def _na_table(rpb, rows):
    nblk = rows // NA_QROWS
    tabs = []
    c = jnp.arange(GRID_W)
    cs = jnp.clip(c - NA_KW // 2, 0, GRID_W - NA_KW)
    vcol = (c[None, :] >= cs[:, None]) & (c[None, :] < cs[:, None] + NA_KW)
    co = jnp.clip(c[None, :] - c[:, None] + NA_KW - 1, 0, 2 * NA_KW - 2)
    for blk in (0, min(1, nblk - 1), nblk - 1):
        r = NA_QROWS * blk + jnp.arange(NA_QROWS)
        kb = min(max(NA_QROWS * blk - NA_KH // 2, 0), rows - NA_KROWS)
        krow = kb + jnp.arange(NA_KROWS)
        rs = jnp.clip(r - NA_KH // 2, 0, rows - NA_KH)
        vrow = (krow[None, :] >= rs[:, None]) & (krow[None, :] < rs[:, None] + NA_KH)
        ro = jnp.clip(krow[None, :] - r[:, None] + NA_KH - 1, 0, 2 * NA_KH - 2)
        bias = rpb[:, ro][:, :, :, co]
        bias = bias.transpose(0, 1, 3, 2, 4)
        valid = vrow[:, None, :, None] & vcol[None, :, None, :]
        tabs.append(jnp.where(valid[None], bias, NEG_BIG).reshape(
            rpb.shape[0], NA_QROWS * GRID_W, NA_KROWS * GRID_W))
    return jnp.stack(tabs).astype(F32)


def _bfly(x, op):
    for sh in (1, 2, 4):
        x = op(x, pltpu.roll(x, sh, axis=x.ndim - 2))
    return x


def _route(h2, rwh_ref, rwl_ref, rb_ref, idx_ref, gate_ref):
    tm = h2.shape[0]
    h_hi = h2.astype(BF16)
    h_lo = (h2 - h_hi.astype(F32)).astype(BF16)
    rwh = rwh_ref[...]
    logits = _dot_nt(rwh, h_hi) + _dot_nt(rwh, h_lo) + _dot_nt(rwl_ref[...], h_hi)
    scores = _sigmoid(logits)
    sel = scores + rb_ref[...]
    sub = lax.broadcasted_iota(jnp.int32, (GROUP_SIZE, tm), 0)
    sc_g = [scores[g * GROUP_SIZE:(g + 1) * GROUP_SIZE] for g in range(N_GROUPS)]
    sel_g = [sel[g * GROUP_SIZE:(g + 1) * GROUP_SIZE] for g in range(N_GROUPS)]

    gscore = []
    for x in sel_g:
        m1 = _bfly(x, jnp.maximum)
        first = _bfly(jnp.where(x == m1, sub, GROUP_SIZE), jnp.minimum)
        m2 = _bfly(jnp.where(sub == first, -jnp.inf, x), jnp.maximum)
        gscore.append(m1 + m2)

    gmask = [jnp.zeros((GROUP_SIZE, tm), jnp.bool_) for _ in range(N_GROUPS)]
    for _ in range(TOPK_GROUPS):
        best = functools.reduce(jnp.maximum, gscore)
        gidx = functools.reduce(jnp.minimum,
                                [jnp.where(gs == best, g, N_GROUPS) for g, gs in enumerate(gscore)])
        for g in range(N_GROUPS):
            hit = gidx == g
            gmask[g] = jnp.logical_or(gmask[g], hit)
            gscore[g] = jnp.where(hit, -jnp.inf, gscore[g])

    cand = [jnp.where(gmask[g], sel_g[g], -jnp.inf) for g in range(N_GROUPS)]
    eid = [sub + g * GROUP_SIZE for g in range(N_GROUPS)]
    idx_out = jnp.zeros((TOP_K, tm), jnp.int32)
    gate_out = jnp.zeros((TOP_K, tm), F32)
    gsum = jnp.zeros((GROUP_SIZE, tm), F32)
    for k in range(TOP_K):
        best = _bfly(functools.reduce(jnp.maximum, cand), jnp.maximum)
        pick = _bfly(functools.reduce(
            jnp.minimum, [jnp.where(cand[g] == best, eid[g], N_EXPERTS) for g in range(N_GROUPS)]),
            jnp.minimum)
        gv = _bfly(functools.reduce(
            jnp.add, [jnp.where(eid[g] == pick, sc_g[g], 0.0) for g in range(N_GROUPS)]), jnp.add)
        cand = [jnp.where(eid[g] == pick, -jnp.inf, cand[g]) for g in range(N_GROUPS)]
        idx_out = jnp.where(sub == k, pick, idx_out)
        gate_out = jnp.where(sub == k, gv, gate_out)
        gsum = gsum + gv
    idx_ref[...] = idx_out
    gate_ref[...] = gate_out / gsum * ROUTED_SCALE


ROW_TILE = 8


def _load_rows(ref, start, n):
    return jnp.concatenate(
        [ref[pl.ds(start + s, n, stride=ROW_TILE), :] for s in range(ROW_TILE)], axis=-1)


def _store_rows(ref, val):
    n, d = val.shape
    w = d // ROW_TILE
    for s in range(ROW_TILE):
        ref[pl.ds(s, n, stride=ROW_TILE), :] = val[:, s * w:(s + 1) * w]


def _tail(x, mix, g1, n2w, sc2, sh2, rwh_ref, rwl_ref, rb_ref, x1_ref, h2_ref, idx_ref, gate_ref):
    x1 = x + g1 * mix
    h2 = _modnorm(x1, n2w, sc2, sh2)
    x1_ref[...] = x1
    _store_rows(h2_ref, h2)
    _route(h2, rwh_ref, rwl_ref, rb_ref, idx_ref, gate_ref)


def _post_mix_kernel(x_ref, mla_ref, na_ref, wo_a_ref, wo_b_ref, g1_ref, n2_ref, sc2_ref, sh2_ref,
                     rwh_ref, rwl_ref, rb_ref, x1_ref, h2_ref, idx_ref, gate_ref):
    mix = _dot(mla_ref[...], wo_a_ref[...]) + _dot(na_ref[...], wo_b_ref[...])
    _tail(x_ref[...], mix, g1_ref[0], n2_ref[...], sc2_ref[0], sh2_ref[0],
          rwh_ref, rwl_ref, rb_ref, x1_ref, h2_ref, idx_ref, gate_ref)


def _tail_specs(t, d, tm, tiles_per_batch):
    tok = lambda w: pl.BlockSpec((tm, w), lambda i: (i, 0))
    mod = pl.BlockSpec((1, 1, d), lambda i: (i // tiles_per_batch, 0, 0))
    full2 = lambda shp: pl.BlockSpec(shp, lambda i: (0, 0))
    in_tail = [mod, full2((1, d)), mod, mod, full2((N_EXPERTS, d)), full2((N_EXPERTS, d)),
               full2((N_EXPERTS, 1))]
    out_specs = [tok(d), pl.BlockSpec((tm * ROW_TILE, d // ROW_TILE), lambda i: (i, 0)),
                 pl.BlockSpec((TOP_K, tm), lambda i: (0, i)),
                 pl.BlockSpec((TOP_K, tm), lambda i: (0, i))]
    out_shape = [jax.ShapeDtypeStruct((t, d), F32),
                 jax.ShapeDtypeStruct((t * ROW_TILE, d // ROW_TILE), F32),
                 jax.ShapeDtypeStruct((TOP_K, t), jnp.int32), jax.ShapeDtypeStruct((TOP_K, t), F32)]
    return tok, mod, full2, in_tail, out_specs, out_shape


def _post_mix(x2d, mla_o, na_o, wo_a, wo_b, g1, n2w, sc2, sh2, rwh, rwl, rb, seq, tm):
    t, d = x2d.shape
    tok, mod, full2, in_tail, out_specs, out_shape = _tail_specs(t, d, tm, seq // tm)
    return pl.pallas_call(
        _post_mix_kernel, grid=(t // tm,),
        in_specs=[tok(d), tok(mla_o.shape[1]), tok(na_o.shape[1]), full2(wo_a.shape), full2(wo_b.shape)]
        + in_tail,
        out_specs=out_specs, out_shape=out_shape, compiler_params=_cparams(1), name="post_mix",
    )(x2d, mla_o, na_o, wo_a, wo_b, g1, n2w, sc2, sh2, rwh, rwl, rb)


def _conv_pre_kernel(x_ref, sh_ref, sc_ref, n1_ref, w_ref, b_ref, g_ref):
    d = x_ref.shape[-1]
    h = _modnorm(x_ref[0], n1_ref[...], sc_ref[0], sh_ref[0])
    a = _dot(h.astype(BF16), w_ref[...]) + b_ref[...]
    g_ref[0] = a[:, :d] * _sigmoid(a[:, d:])


def _conv_pre(x, sh, sc, n1w, w, bias, tm):
    b, s, d = x.shape
    mod = pl.BlockSpec((1, 1, d), lambda bi, si: (bi, 0, 0))
    full = lambda a: pl.BlockSpec(a.shape, lambda bi, si: (0,) * a.ndim)
    tok = pl.BlockSpec((1, tm, d), lambda bi, si: (bi, si, 0))
    return pl.pallas_call(
        _conv_pre_kernel, grid=(b, s // tm),
        in_specs=[tok, mod, mod, full(n1w), full(w), full(bias)],
        out_specs=tok, out_shape=jax.ShapeDtypeStruct((b, s, d), F32),
        compiler_params=_cparams(2), name="conv_pre",
    )(x, sh, sc, n1w, w, bias)


def _conv_post_kernel(x_ref, g_ref, gp_ref, gn_ref, dw_ref, dwb_ref, lnw_ref, lnb_ref, w2_ref, b2_ref,
                      g1_ref, n2_ref, sc2_ref, sh2_ref, rwh_ref, rwl_ref, rb_ref,
                      x1_ref, h2_ref, idx_ref, gate_ref, gext_ref, gsh_ref, y_ref, *, tiles_per_batch):
    tm = x_ref.shape[0]
    si = pl.program_id(0) % tiles_per_batch
    zero = jnp.zeros_like(gp_ref[0])
    gext_ref[0:CONV_HALO] = jnp.where(si > 0, gp_ref[0], zero)
    gext_ref[CONV_HALO:CONV_HALO + tm] = g_ref[0]
    gext_ref[CONV_HALO + tm:] = jnp.where(si < tiles_per_batch - 1, gn_ref[0], zero)
    base = CONV_HALO - CONV_W // 2
    span = (base + CONV_W - 1) // SUBLANES * SUBLANES
    for r in range(1, SUBLANES):
        gsh_ref[r - 1] = gext_ref[r:r + span + tm, :]

    def chunk(c, carry):
        r0 = pl.multiple_of(c * CONV_CHUNK, CONV_CHUNK)
        acc = jnp.zeros((CONV_CHUNK, x_ref.shape[1]), F32) + dwb_ref[...]
        for r in range(SUBLANES):
            src = gext_ref if r == 0 else gsh_ref.at[r - 1]
            for a in range(0, span + 1, SUBLANES):
                j = a + r - base
                if 0 <= j < CONV_W:
                    acc = acc + dw_ref[j:j + 1, :] * src[pl.ds(r0 + a, CONV_CHUNK), :]
        y_ref[pl.ds(r0, CONV_CHUNK), :] = acc
        return carry

    lax.fori_loop(0, tm // CONV_CHUNK, chunk, 0)
    y = y_ref[...]
    mu = jnp.mean(y, axis=-1, keepdims=True)
    yc = y - mu
    var = jnp.mean(yc * yc, axis=-1, keepdims=True)
    z = _silu(yc * lax.rsqrt(var + EPS) * lnw_ref[...] + lnb_ref[...])
    mix = _dot(z.astype(BF16), w2_ref[...]) + b2_ref[...]
    _tail(x_ref[...], mix, g1_ref[0], n2_ref[...], sc2_ref[0], sh2_ref[0],
          rwh_ref, rwl_ref, rb_ref, x1_ref, h2_ref, idx_ref, gate_ref)


def _conv_post(x2d, g, dw, dwb, lnw, lnb, w2, b2, g1, n2w, sc2, sh2, rwh, rwl, rb, tm):
    t, d = x2d.shape
    b, s, _ = g.shape
    tpb = s // tm
    hb = tm // CONV_HALO
    nhb = s // CONV_HALO
    tok, mod, full2, in_tail, out_specs, out_shape = _tail_specs(t, d, tm, tpb)
    g_main = pl.BlockSpec((1, tm, d), lambda i: (i // tpb, i % tpb, 0))
    g_prev = pl.BlockSpec((1, CONV_HALO, d),
                          lambda i: (i // tpb, jnp.maximum((i % tpb) * hb - 1, 0), 0))
    g_next = pl.BlockSpec((1, CONV_HALO, d),
                          lambda i: (i // tpb, jnp.minimum((i % tpb + 1) * hb, nhb - 1), 0))
    return pl.pallas_call(
        functools.partial(_conv_post_kernel, tiles_per_batch=tpb), grid=(t // tm,),
        in_specs=[tok(d), g_main, g_prev, g_next, full2(dw.shape), full2(dwb.shape), full2(lnw.shape),
                  full2(lnb.shape), full2(w2.shape), full2(b2.shape)] + in_tail,
        out_specs=out_specs, out_shape=out_shape,
        scratch_shapes=[pltpu.VMEM((tm + 2 * CONV_HALO, d), F32),
                        pltpu.VMEM((SUBLANES - 1, tm + 2 * CONV_HALO - SUBLANES, d), F32),
                        pltpu.VMEM((tm, d), F32)],
        compiler_params=_cparams(1), name="conv_post",
    )(x2d, g, g, g, dw, dwb, lnw, lnb, w2, b2, g1, n2w, sc2, sh2, rwh, rwl, rb)


GATHER_UNROLL = 8


def _gather_tiles(idx_ref, src_ref, dst_ref, sem, n):
    def body(c, carry):
        for u in range(GATHER_UNROLL):
            j = c * GATHER_UNROLL + u
            s = idx_ref[0, 0, j]
            pltpu.make_async_copy(
                src_ref.at[pl.ds(pl.multiple_of(s * ROW_TILE, ROW_TILE), ROW_TILE)],
                dst_ref.at[pl.ds(pl.multiple_of(j * ROW_TILE, ROW_TILE), ROW_TILE)], sem).start()
        return carry
    lax.fori_loop(0, n // GATHER_UNROLL, body, 0)


def _gather_step(cur_idx_ref, next_idx_ref, src_ref, buf, sem, n):
    i = pl.program_id(0)
    slot = lax.rem(i, 2)

    @pl.when(i == 0)
    def _():
        _gather_tiles(cur_idx_ref, src_ref, buf.at[0], sem.at[0], n)

    @pl.when(i + 1 < pl.num_programs(0))
    def _():
        _gather_tiles(next_idx_ref, src_ref, buf.at[1 - slot], sem.at[1 - slot], n)

    pltpu.make_async_copy(src_ref.at[pl.ds(0, n * ROW_TILE)], buf.at[slot], sem.at[slot]).wait()
    return buf.at[slot]


def _gather_specs(steps, n):
    spec = lambda f: pl.BlockSpec((1, 1, n), f, memory_space=pltpu.SMEM)
    return [spec(lambda i, *_: (i, 0, 0)), spec(lambda i, *_: (jnp.minimum(i + 1, steps - 1), 0, 0))]


def _experts_kernel(blk_e_ref, tcur_ref, tnext_ref, h2_ref, wgu_ref, wd_ref, ys_ref, xbuf, sem):
    xcur = _gather_step(tcur_ref, tnext_ref, h2_ref, xbuf, sem, MOE_BLOCK)
    x = _load_rows(xcur, 0, MOE_BLOCK).astype(BF16)
    gu = _dot(x, wgu_ref[0])
    u = _silu(gu[:, :D_EXPERT]) * gu[:, D_EXPERT:]
    _store_rows(ys_ref, _dot(u.astype(BF16), wd_ref[0]))


def _experts(h2r, tok_slot, blk_e, wgu, wd):
    w = h2r.shape[1]
    d = w * ROW_TILE
    nb = blk_e.shape[0]
    rows = MOE_BLOCK * ROW_TILE
    tok3 = tok_slot.reshape(nb, 1, MOE_BLOCK)
    grid_spec = pltpu.PrefetchScalarGridSpec(
        num_scalar_prefetch=1, grid=(nb,),
        in_specs=_gather_specs(nb, MOE_BLOCK) + [
            pl.BlockSpec(memory_space=pl.ANY),
            pl.BlockSpec((1, d, 2 * D_EXPERT), lambda i, be: (be[i], 0, 0)),
            pl.BlockSpec((1, D_EXPERT, d), lambda i, be: (be[i], 0, 0))],
        out_specs=pl.BlockSpec((rows, w), lambda i, be: (i, 0)),
        scratch_shapes=[pltpu.VMEM((2, rows, w), F32), pltpu.SemaphoreType.DMA((2,))])
    return pl.pallas_call(
        _experts_kernel, grid_spec=grid_spec, out_shape=jax.ShapeDtypeStruct((nb * rows, w), F32),
        compiler_params=_cparams(1), name="moe_experts",
    )(blk_e, tok3, tok3, h2r, wgu, wd)


RUN_CHUNK = 16


def _run_copies(c_ref, ys_ref, buf, sem, wait):
    rows = RUN_CHUNK * ROW_TILE

    def per_chunk(c, carry):
        if wait:
            pltpu.make_async_copy(ys_ref.at[pl.ds(0, rows)], buf.at[pl.ds(0, rows)], sem).wait()
        else:
            src = pl.multiple_of(c_ref[0, 0, 1 + c] * ROW_TILE, ROW_TILE)
            dst = pl.multiple_of(c * rows, rows)
            pltpu.make_async_copy(ys_ref.at[pl.ds(src, rows)], buf.at[pl.ds(dst, rows)], sem).start()
        return carry

    lax.fori_loop(0, c_ref[0, 0, 0], per_chunk, 0)


def _combine_kernel(mcur_ref, mnext_ref, loc_ref, gate_ref, x1_ref, h2_ref, g2_ref, wsgu_ref, wsd_ref,
                    ys_ref, o_ref, ybuf, yt_ref, sem):
    i = pl.program_id(0)
    slot = lax.rem(i, 2)
    tm = x1_ref.shape[0]

    @pl.when(i == 0)
    def _():
        _run_copies(mcur_ref, ys_ref, ybuf.at[0], sem.at[0], wait=False)

    @pl.when(i + 1 < pl.num_programs(0))
    def _():
        _run_copies(mnext_ref, ys_ref, ybuf.at[1 - slot], sem.at[1 - slot], wait=False)

    _run_copies(mcur_ref, ys_ref, ybuf.at[slot], sem.at[slot], wait=True)
    ycur = ybuf.at[slot]

    def per_token(t, carry):
        acc = jnp.zeros((ROW_TILE, yt_ref.shape[1]), F32)
        for k in range(TOP_K):
            j = k * tm + t
            r = pl.multiple_of(loc_ref[0, 0, j] * ROW_TILE, ROW_TILE)
            acc = acc + gate_ref[0, 0, j] * ycur[pl.ds(r, ROW_TILE), :]
        yt_ref[pl.ds(pl.multiple_of(t * ROW_TILE, ROW_TILE), ROW_TILE), :] = acc
        return carry

    lax.fori_loop(0, tm, per_token, 0)
    y = _load_rows(yt_ref, 0, tm)
    hb = _load_rows(h2_ref, 0, tm).astype(BF16)
    gu = _dot(hb, wsgu_ref[...])
    ds = wsd_ref.shape[0]
    u = _silu(gu[:, :ds]) * gu[:, ds:]
    y = y + _dot(u.astype(BF16), wsd_ref[...])
    o_ref[...] = x1_ref[...] + g2_ref[0] * y


def _combine(x1, h2r, ys, meta, loc_kt, gate_kt, g2, wsgu, wsd, seq, tm):
    t, d = x1.shape
    w = d // ROW_TILE
    tpb = seq // tm
    steps = t // tm
    n_meta = meta.shape[-1]
    cap = (n_meta - 1) * RUN_CHUNK * ROW_TILE
    tok = pl.BlockSpec((tm, d), lambda i: (i, 0))
    full2 = lambda a: pl.BlockSpec(a.shape, lambda i: (0, 0))
    smem = lambda n, f: pl.BlockSpec((1, 1, n), f, memory_space=pltpu.SMEM)
    cur = lambda i: (i, 0, 0)
    return pl.pallas_call(
        _combine_kernel, grid=(steps,),
        in_specs=[smem(n_meta, cur),
                  smem(n_meta, lambda i: (jnp.minimum(i + 1, steps - 1), 0, 0)),
                  smem(TOP_K * tm, cur), smem(TOP_K * tm, cur),
                  tok, pl.BlockSpec((tm * ROW_TILE, w), lambda i: (i, 0)),
                  pl.BlockSpec((1, 1, d), lambda i: (i // tpb, 0, 0)), full2(wsgu), full2(wsd),
                  pl.BlockSpec(memory_space=pl.ANY)],
        out_specs=tok, out_shape=jax.ShapeDtypeStruct((t, d), F32),
        scratch_shapes=[pltpu.VMEM((2, cap, w), F32), pltpu.VMEM((tm * ROW_TILE, w), F32),
                        pltpu.SemaphoreType.DMA((2,))],
        compiler_params=_cparams(1), name="moe_combine",
    )(meta, meta, loc_kt, gate_kt, x1, h2r, g2, wsgu, wsd, ys)


def _moe(x1, h2r, idx, gates, g2, wgu, wd, wsgu, wsd, seq, tm):
    t, d = x1.shape
    a = TOP_K * t
    n_blocks = a // MOE_BLOCK + N_EXPERTS + 1
    eq = idx[:, :, None] == jnp.arange(N_EXPERTS, dtype=jnp.int32)
    onehot = eq.sum(0).astype(jnp.int32)
    before = jnp.cumsum(onehot, axis=0) - onehot
    counts = onehot.sum(0)
    padded = (counts + MOE_BLOCK - 1) // MOE_BLOCK * MOE_BLOCK
    pend = jnp.cumsum(padded)
    pstart = pend - padded
    blk_pos = jnp.arange(n_blocks, dtype=jnp.int32) * MOE_BLOCK
    blk_e = jnp.minimum((pend[None, :] <= blk_pos[:, None]).sum(-1), N_EXPERTS - 1).astype(jnp.int32)
    tok_ids = jnp.arange(t, dtype=jnp.int32)
    by_expert = lax.rem(jnp.sort((idx * t + tok_ids[None, :]).reshape(a)), t)
    start = jnp.cumsum(counts) - counts
    rank = blk_pos[:, None] + jnp.arange(MOE_BLOCK, dtype=jnp.int32)[None, :] - pstart[blk_e][:, None]
    valid = rank < counts[blk_e][:, None]
    tok_slot = jnp.where(valid, by_expert[jnp.clip(start[blk_e][:, None] + rank, 0, a - 1)], 0)

    ys = _experts(h2r, tok_slot.astype(jnp.int32), blk_e, wgu, wd)

    nt = t // tm
    before_t = before.reshape(nt, tm, N_EXPERTS)
    first = before_t[:, 0, :]
    n_run = onehot.reshape(nt, tm, N_EXPERTS).sum(1)
    n_chunk = (n_run + RUN_CHUNK - 1) // RUN_CHUNK
    chunk_end = jnp.cumsum(n_chunk, axis=1)
    chunk_off = chunk_end - n_chunk
    buf_off = chunk_off * RUN_CHUNK
    max_chunks = TOP_K * tm // RUN_CHUNK + N_EXPERTS
    c = jnp.arange(max_chunks, dtype=jnp.int32)[None, :, None]
    owner = (chunk_off[:, None, :] <= c) & (c < chunk_end[:, None, :])
    piece = (pstart[None, :] + first)[:, None, :] + (c - chunk_off[:, None, :]) * RUN_CHUNK
    piece_slot = jnp.where(owner, piece, 0).sum(-1)
    meta = jnp.concatenate([chunk_end[:, -1:], piece_slot], axis=1).astype(jnp.int32)
    meta = meta.reshape(nt, 1, 1 + max_chunks)
    loc_e = (before_t - first[:, None, :] + buf_off[:, None, :]).reshape(t, N_EXPERTS)
    loc = jnp.where(eq, loc_e[None], 0).sum(-1).astype(jnp.int32)
    k_major = lambda v: v.reshape(TOP_K, nt, tm).transpose(1, 0, 2).reshape(nt, 1, TOP_K * tm)
    return _combine(x1, h2r, ys, meta, k_major(loc), k_major(gates), g2, wsgu, wsd, seq, tm)


def _axial_tables(seq, wq, wk):
    t = jnp.arange(seq)
    row = (t // GRID_W).astype(F32)
    col = (t % GRID_W).astype(F32)
    n_freq = MLA_ROPE // 4
    inv = ROPE_THETA ** (-jnp.arange(n_freq, dtype=F32) / n_freq)
    ang = jnp.concatenate([row[:, None] * inv, col[:, None] * inv], axis=-1)
    cos, sin = jnp.cos(ang), jnp.sin(ang)
    half = MLA_ROPE // 2
    pad = jnp.zeros((seq, HEAD_PAD - MLA_QK), F32)

    def tables(w, f):
        w0, w1, w2 = w[:MLA_NOPE], w[MLA_NOPE:MLA_NOPE + half], w[MLA_NOPE + half:]
        c_t = jnp.concatenate([jnp.broadcast_to(w0, (seq, MLA_NOPE)), w1 * cos, w2 * cos, pad], axis=-1)
        s_t = jnp.concatenate([jnp.zeros((seq, MLA_NOPE), F32), -w2 * sin, w1 * sin, pad], axis=-1)
        return c_t * f, s_t * f

    cq, sq = tables(wq, MLA_QK ** -0.5 * math.log2(math.e))
    ck, sk = tables(wk, 1.0)
    return cq, sq, ck, sk


def _attn_params(seq, w_in, qa_w, kva_w, w_q_b, w_kv_b, mq_w, mk_w, naq_w, nak_w):
    d = w_in.shape[0]
    half = MLA_ROPE // 2
    o_pe = MLA_Q_LORA + MLA_KV_LORA
    o_na = o_pe + MLA_ROPE
    z = lambda n: jnp.zeros((d, n), F32)
    pe1, pe2 = w_in[:, o_pe:o_pe + half], w_in[:, o_pe + half:o_na]
    pe = jnp.concatenate([z(MLA_NOPE), pe1, pe2, z(HEAD_PAD - MLA_QK)], axis=1)
    pes = jnp.concatenate([z(MLA_NOPE), pe2, pe1, z(HEAD_PAD - MLA_QK)], axis=1)
    w_in_r = jnp.concatenate([w_in[:, :o_pe], pe, pes, w_in[:, o_na:]], axis=1).astype(BF16)

    wq = w_q_b.reshape(MLA_Q_LORA, MLA_HEADS, MLA_QK)
    zq = lambda n: jnp.zeros((MLA_Q_LORA, MLA_HEADS, n), F32)
    wq_a = jnp.concatenate([wq, zq(HEAD_PAD - MLA_QK)], axis=-1)
    wq_b = jnp.concatenate([zq(MLA_NOPE), wq[..., MLA_NOPE + half:], wq[..., MLA_NOPE:MLA_NOPE + half],
                            zq(HEAD_PAD - MLA_QK)], axis=-1)
    wkv = w_kv_b.reshape(MLA_KV_LORA, MLA_HEADS, MLA_NOPE + MLA_V)
    zk = jnp.zeros((MLA_KV_LORA, MLA_HEADS, HEAD_PAD - MLA_NOPE), F32)
    wk = jnp.concatenate([wkv[..., :MLA_NOPE], zk], axis=-1)
    wv = jnp.concatenate([wkv[..., MLA_NOPE:], zk], axis=-1)
    hw = MLA_HEADS * HEAD_PAD
    cq, sq, ck, sk = _axial_tables(seq, mq_w, mk_w)
    return {
        "w_in": w_in_r, "qa_w": qa_w.reshape(1, -1), "kva_w": kva_w.reshape(1, -1),
        "wq_a": wq_a.reshape(MLA_Q_LORA, hw).astype(BF16), "wq_b": wq_b.reshape(MLA_Q_LORA, hw).astype(BF16),
        "wk": wk.reshape(MLA_KV_LORA, hw).astype(BF16), "wv": wv.reshape(MLA_KV_LORA, hw).astype(BF16),
        "cq": cq, "sq": sq, "ck": ck, "sk": sk,
        "na_qw": (jnp.tile(naq_w, NA_HEADS) * NA_HD ** -0.5).reshape(1, -1),
        "na_kw": jnp.tile(nak_w, NA_HEADS).reshape(1, -1),
    }


def _split_hi_lo(w):
    hi = w.astype(BF16)
    return hi, (w - hi.astype(F32)).astype(BF16)


TOKEN_TILE = 512
CONV_TILE = 256
COMBINE_TILE = 256
FLASH_TQ = 512
FLASH_TK = 2048


def kernel(x, c, norm1_w, norm2_w, ada_w, ada_b, w_in, q_a_norm_w, kv_a_norm_w, w_q_b, w_kv_b,
           mla_q_norm_w, mla_k_norm_w, na_q_norm_w, na_k_norm_w, na_rpb, w_out, pw1_w, pw1_b, dw_w, dw_b,
           conv_ln_w, conv_ln_b, pw2_w, pw2_b, router_w, router_b, exp_w_gate, exp_w_up, exp_w_down,
           sh_w_gate, sh_w_up, sh_w_down):
    b, s, d = x.shape
    depth = ada_w.shape[0]
    t = b * s
    mod = _ada_mod(c, ada_w, ada_b)
    row = lambda v: v.reshape(1, -1)
    for l in range(depth):
        m6 = mod[l].reshape(b, 1, 6, d)
        sh1, sc1, g1, sh2, sc2, g2 = (m6[:, :, i] for i in range(6))
        i = l // 2
        rwh, rwl = _split_hi_lo(router_w[l].T)
        rb = router_b[l].reshape(N_EXPERTS, 1)
        tail_args = (g1, row(norm2_w[l]), sc2, sh2, rwh, rwl, rb)
        if l % 2 == 0:
            prm = _attn_params(s, w_in[i], q_a_norm_w[i], kv_a_norm_w[i], w_q_b[i], w_kv_b[i],
                               mla_q_norm_w[i], mla_k_norm_w[i], na_q_norm_w[i], na_k_norm_w[i])
            qm, km, vm, qn, kn, vn = _attn_pre(x, sh1, sc1, row(norm1_w[l]), prm, TOKEN_TILE)
            mla_o = _mla_flash(qm, km, vm, FLASH_TQ, FLASH_TK)
            na_o = _na_attention(qn, kn, vn, _na_table(na_rpb[i], s // GRID_W))
            wo = w_out[i].astype(BF16)
            nm = MLA_HEADS * MLA_V
            x1, h2, idx, gates = _post_mix(x.reshape(t, d), mla_o.reshape(t, -1), na_o.reshape(t, -1),
                                           wo[:nm], wo[nm:], *tail_args, s, TOKEN_TILE)
        else:
            g = _conv_pre(x, sh1, sc1, row(norm1_w[l]), pw1_w[i].astype(BF16), row(pw1_b[i]), TOKEN_TILE)
            x1, h2, idx, gates = _conv_post(x.reshape(t, d), g, dw_w[i], row(dw_b[i]), row(conv_ln_w[i]),
                                            row(conv_ln_b[i]), pw2_w[i].astype(BF16), row(pw2_b[i]),
                                            *tail_args, CONV_TILE)
        wgu = jnp.concatenate([exp_w_gate[l], exp_w_up[l]], axis=-1).astype(BF16)
        wsgu = jnp.concatenate([sh_w_gate[l], sh_w_up[l]], axis=-1).astype(BF16)
        x = _moe(x1, h2, idx, gates, g2, wgu, exp_w_down[l].astype(BF16), wsgu,
                 sh_w_down[l].astype(BF16), s, COMBINE_TILE).reshape(b, s, d)
    return x
```
